```python
import jax
import jax.numpy as jnp
from jax import lax
import numpy as np

D_MODEL = 1024
BATCH = 2
SEQ = 8192
DEPTH = 2
DEC_BATCH = 32
DEC_SEQ = 4
PAST_LEN = 8192
PAGE_SIZE = 128

N_EVEN = (DEPTH + 1) // 2
N_ODD = DEPTH // 2
A_HEADS = 4
A_HEAD_DIM = D_MODEL // 8
A_WIDTH = A_HEADS * A_HEAD_DIM
A_CHUNK = 128
B_HEADS = 4
B_HEAD_DIM = D_MODEL // 8
B_WIDTH = B_HEADS * B_HEAD_DIM
B_CHUNK = 64
C_HEADS = 16
C_HEAD_DIM = D_MODEL // C_HEADS
C_WIDTH = C_HEADS * C_HEAD_DIM
C_QBLOCK = 128
C_BIAS_INIT = -7.0
FFN_HIDDEN = -(-8 * D_MODEL // (3 * 256)) * 256
DN_ALPHA = (2 * DEPTH) ** 0.25
DN_BETA = (8 * DEPTH) ** -0.25
LN_EPS = 1e-5
RMS_EPS = 1e-6
AB_SPLITS = [A_WIDTH, 2 * A_WIDTH, 2 * A_WIDTH + B_WIDTH, 2 * A_WIDTH + 2 * B_WIDTH, 2 * A_WIDTH + 3 * B_WIDTH]
AB_IN = 2 * A_WIDTH + 4 * B_WIDTH

kernel_name = 'hybrid_gmlp_hgrn2_stickbreak_step'


def layer_norm(x, g, b):
    xf = x.astype(jnp.float32)
    mu = jnp.mean(xf, -1, keepdims=True)
    var = jnp.mean(jnp.square(xf - mu), -1, keepdims=True)
    return ((xf - mu) * lax.rsqrt(var + LN_EPS) * g.astype(jnp.float32) + b.astype(jnp.float32)).astype(x.dtype)


def swiglu(h, wg, wu, wd):
    return (jax.nn.silu(h @ wg) * (h @ wu)) @ wd


def chunk_gmlp(u, v, w_s, b_s, ln_g, ln_b):
    N, L, _ = u.shape
    vh = layer_norm(v.reshape(N, L, A_HEADS, A_HEAD_DIM), ln_g.reshape(A_HEADS, A_HEAD_DIM), ln_b.reshape(A_HEADS, A_HEAD_DIM))
    Lp = -(-L // A_CHUNK) * A_CHUNK
    vc = jnp.pad(vh, ((0, 0), (0, Lp - L), (0, 0), (0, 0))).reshape(N, Lp // A_CHUNK, A_CHUNK, A_HEADS, A_HEAD_DIM)
    w = jnp.where(jnp.tril(jnp.ones((A_CHUNK, A_CHUNK), bool)), w_s, 0)
    sv = jnp.einsum('hts,ncshd->ncthd', w.astype(vc.dtype), vc) + b_s.T[None, None, :, :, None]
    sv = sv.reshape(N, Lp, A_WIDTH)[:, :L]
    return u * sv, vh.reshape(N, L, A_WIDTH)


def hgrn2_chunked(q, k, v, log_f, s0):
    N, L, H, _ = q.shape
    c = min(B_CHUNK, L)
    Lp = -(-L // c) * c
    n_chunks = Lp // c
    pad = ((0, 0), (0, Lp - L), (0, 0), (0, 0))

    def chunks(a):
        a = jnp.pad(a, pad)
        return a.reshape(N, n_chunks, c, H, a.shape[-1]).transpose(1, 0, 3, 2, 4)

    causal = jnp.tril(jnp.ones((c, c), bool))[:, :, None]

    def step(S, inp):
        qc, kc, vc, lf = inp
        G = jnp.cumsum(lf, axis=2)
        o_inter = jnp.einsum('nhtk,nhkv->nhtv', qc * jnp.exp(G), S)
        decay = jnp.exp(jnp.where(causal, G[:, :, :, None, :] - G[:, :, None, :, :], -jnp.inf))
        scores = jnp.einsum('nhtk,nhsk,nhtsk->nhts', qc, kc, decay)
        o_intra = jnp.einsum('nhts,nhsv->nhtv', scores, vc)
        G_last = G[:, :, -1:, :]
        S_new = jnp.exp(G_last[:, :, 0, :])[..., None] * S + jnp.einsum('nhsk,nhsv->nhkv', kc * jnp.exp(G_last - G), vc)
        return S_new, o_inter + o_intra

    S_fin, o = lax.scan(step, s0, (chunks(q), chunks(k), chunks(v), chunks(log_f)))
    o = o.transpose(1, 0, 3, 2, 4).reshape(N, Lp, H, v.shape[-1])[:, :L]
    return o, S_fin


def mixer_ab(h, s0, w_in, w_out, a_w_s, a_b_s, a_ln_g, a_ln_b, lb, b_norm_g):
    N, L, _ = h.shape
    u, v, bq, bf, bi, bg = jnp.split(h @ w_in, AB_SPLITS, axis=-1)
    a_out, a_v = chunk_gmlp(jax.nn.gelu(u), jax.nn.gelu(v), a_w_s, a_b_s, a_ln_g, a_ln_b)
    heads = lambda t: t.astype(jnp.float32).reshape(N, L, B_HEADS, B_HEAD_DIM)
    f_pre = heads(bf)
    lb = lb.reshape(B_HEADS, B_HEAD_DIM)
    log_f = jnp.log(lb + (1.0 - lb) * jax.nn.sigmoid(f_pre))
    k = (1.0 - lb) * jax.nn.sigmoid(-f_pre)
    o, s_new = hgrn2_chunked(jax.nn.silu(heads(bq)), k, heads(bi), log_f, s0.astype(jnp.float32))
    o = o * lax.rsqrt(jnp.mean(o * o, -1, keepdims=True) + RMS_EPS) * b_norm_g.astype(jnp.float32)
    o = (o.reshape(N, L, B_WIDTH) * jax.nn.silu(bg.astype(jnp.float32))).astype(h.dtype)
    y = jnp.concatenate([a_out, o], axis=-1) @ w_out
    return y, a_v, s_new.astype(h.dtype)


def stick_breaking(q, k, v, bias, q_pos, k_pos):
    z = jnp.einsum('nqhd,nkhd->nhqk', q, k).astype(jnp.float32) * (C_HEAD_DIM ** -0.5) + bias.astype(jnp.float32)[None, :, None, None]
    mask = (k_pos[None, :] < q_pos[:, None])[None, None]
    log_beta = jax.nn.log_sigmoid(z)
    log_1m = jnp.where(mask, jax.nn.log_sigmoid(-z), 0.0)
    rest = lax.cumsum(log_1m, axis=3, reverse=True) - log_1m
    w = jnp.where(mask, jnp.exp(log_beta + rest), 0.0)
    return jnp.einsum('nhqk,nkhd->nqhd', w, v.astype(jnp.float32)).astype(q.dtype)


def stick_breaking_blocks(q, k, v, bias, q_pos, k_pos):
    N, Lq, H, dh = q.shape
    qb = min(C_QBLOCK, Lq)
    Lp = -(-Lq // qb) * qb
    nb = Lp // qb
    qblocks = jnp.pad(q, ((0, 0), (0, Lp - Lq), (0, 0), (0, 0))).reshape(N, nb, qb, H, dh).transpose(1, 0, 2, 3, 4)
    pblocks = jnp.pad(q_pos, (0, Lp - Lq), mode='edge').reshape(nb, qb)
    out = lax.map(lambda a: stick_breaking(a[0], k, v, bias, a[1], k_pos), (qblocks, pblocks))
    return out.transpose(1, 0, 2, 3, 4).reshape(N, Lp, H, dh)[:, :Lq]


def qkv_heads(h, w_in):
    N, L, _ = h.shape
    q, k, v = jnp.split(h @ w_in, 3, axis=-1)
    r = lambda t: t.reshape(N, L, C_HEADS, C_HEAD_DIM)
    return r(q), r(k), r(v)


def mixer_c_prompt(h, w_in, w_out, bias):
    N, L, _ = h.shape
    q, k, v = qkv_heads(h, w_in)
    pos = jnp.arange(L, dtype=jnp.int32)
    o = stick_breaking_blocks(q, k, v, bias, pos, pos)
    return o.reshape(N, L, C_WIDTH) @ w_out, k, v


def mixer_c_sample(h, pool_k, pool_v, page_table, w_in, w_out, bias):
    N, L, _ = h.shape
    q, k, v = qkv_heads(h, w_in)
    past = page_table.shape[1] * PAGE_SIZE
    k_past = jnp.take(pool_k, page_table, axis=0).reshape(N, past, C_HEADS, C_HEAD_DIM).astype(k.dtype)
    v_past = jnp.take(pool_v, page_table, axis=0).reshape(N, past, C_HEADS, C_HEAD_DIM).astype(v.dtype)
    k_all = jnp.concatenate([k_past, k], axis=1)
    v_all = jnp.concatenate([v_past, v], axis=1)
    k_pos = jnp.arange(past + L, dtype=jnp.int32)
    q_pos = past + jnp.arange(L, dtype=jnp.int32)
    o = stick_breaking_blocks(q, k_all, v_all, bias, q_pos, k_pos)
    return o.reshape(N, L, C_WIDTH) @ w_out, k, v


def setup_inputs(seed: int = 0) -> dict:
    key = jax.random.key(seed)
    ks = jax.random.split(key, 24)
    f32 = jnp.float32
    n = lambda i, shape: jax.random.normal(ks[i], shape, f32)
    n_pages = PAST_LEN // PAGE_SIZE
    n_pool = (DEC_BATCH * n_pages * 5) // 4
    page_table = jax.random.permutation(ks[5], n_pool)[:DEC_BATCH * n_pages].reshape(DEC_BATCH, n_pages).astype(jnp.int32)
    return {
        'x_prompt': n(0, (BATCH, SEQ, D_MODEL)),
        'x_sample': n(1, (DEC_BATCH, DEC_SEQ, D_MODEL)),
        'state_b': n(2, (N_EVEN, DEC_BATCH, B_HEADS, B_HEAD_DIM, B_HEAD_DIM)),
        'cache_k': n(3, (N_ODD, n_pool, PAGE_SIZE, C_HEADS, C_HEAD_DIM)),
        'cache_v': n(4, (N_ODD, n_pool, PAGE_SIZE, C_HEADS, C_HEAD_DIM)),
        'page_table': page_table,
        'ab_w_in': n(6, (N_EVEN, D_MODEL, AB_IN)) * D_MODEL ** -0.5,
        'ab_w_out': n(7, (N_EVEN, A_WIDTH + B_WIDTH, D_MODEL)) * ((A_WIDTH + B_WIDTH) ** -0.5 * DN_BETA),
        'a_w_s': n(8, (N_EVEN, A_HEADS, A_CHUNK, A_CHUNK)) * A_CHUNK ** -0.5,
        'a_b_s': 1.0 + 0.1 * n(9, (N_EVEN, A_HEADS, A_CHUNK)),
        'a_ln_g': 1.0 + 0.1 * n(10, (N_EVEN, A_WIDTH)),
        'a_ln_b': 0.1 * n(11, (N_EVEN, A_WIDTH)),
        'b_lb_logits': 0.5 * n(12, (DEPTH + 1, B_WIDTH)),
        'b_norm_g': 1.0 + 0.1 * n(13, (N_EVEN, B_HEAD_DIM)),
        'c_w_in': n(14, (N_ODD, D_MODEL, 3 * C_WIDTH)) * D_MODEL ** -0.5,
        'c_w_out': n(15, (N_ODD, C_WIDTH, D_MODEL)) * (C_WIDTH ** -0.5 * DN_BETA),
        'c_logit_bias': C_BIAS_INIT + 0.3 * n(23, (N_ODD, C_HEADS)),
        'ln1_g': 1.0 + 0.1 * n(16, (DEPTH, D_MODEL)),
        'ln1_b': 0.1 * n(17, (DEPTH, D_MODEL)),
        'ln2_g': 1.0 + 0.1 * n(18, (DEPTH, D_MODEL)),
        'ln2_b': 0.1 * n(19, (DEPTH, D_MODEL)),
        'ffn_w_gate': n(20, (DEPTH, D_MODEL, FFN_HIDDEN)) * D_MODEL ** -0.5,
        'ffn_w_up': n(21, (DEPTH, D_MODEL, FFN_HIDDEN)) * D_MODEL ** -0.5,
        'ffn_w_down': n(22, (DEPTH, FFN_HIDDEN, D_MODEL)) * (FFN_HIDDEN ** -0.5 * DN_BETA),
    }


def reference(x_prompt, x_sample, state_b, cache_k, cache_v, page_table, ab_w_in, ab_w_out, a_w_s, a_b_s, a_ln_g, a_ln_b, b_lb_logits, b_norm_g, c_w_in, c_w_out, c_logit_bias, ln1_g, ln1_b, ln2_g, ln2_b, ffn_w_gate, ffn_w_up, ffn_w_down):
    hp, hs = x_prompt, x_sample
    lower_bounds = jnp.cumsum(jax.nn.softmax(b_lb_logits.astype(jnp.float32), axis=0), axis=0)
    a_v_s, b_p, b_s, k_p, v_p, k_s, v_s = [], [], [], [], [], [], []
    for layer in range(DEPTH):
        i = layer // 2
        if layer % 2 == 0:
            ab = (ab_w_in[i], ab_w_out[i], a_w_s[i], a_b_s[i], a_ln_g[i], a_ln_b[i], lower_bounds[layer], b_norm_g[i])
            s0 = jnp.zeros((hp.shape[0], B_HEADS, B_HEAD_DIM, B_HEAD_DIM), jnp.float32)
            mp, _, sp = mixer_ab(hp, s0, *ab)
            ms, av, ss = mixer_ab(hs, state_b[i], *ab)
            a_v_s.append(av)
            b_p.append(sp)
            b_s.append(ss)
        else:
            mp, kp, vp = mixer_c_prompt(hp, c_w_in[i], c_w_out[i], c_logit_bias[i])
            ms, kn, vn = mixer_c_sample(hs, cache_k[i], cache_v[i], page_table, c_w_in[i], c_w_out[i], c_logit_bias[i])
            k_p.append(kp)
            v_p.append(vp)
            k_s.append(kn)
            v_s.append(vn)
        hp = layer_norm(DN_ALPHA * hp + mp, ln1_g[layer], ln1_b[layer])
        hs = layer_norm(DN_ALPHA * hs + ms, ln1_g[layer], ln1_b[layer])
        hp = layer_norm(DN_ALPHA * hp + swiglu(hp, ffn_w_gate[layer], ffn_w_up[layer], ffn_w_down[layer]), ln2_g[layer], ln2_b[layer])
        hs = layer_norm(DN_ALPHA * hs + swiglu(hs, ffn_w_gate[layer], ffn_w_up[layer], ffn_w_down[layer]), ln2_g[layer], ln2_b[layer])
    state_a_v_sample = jnp.stack(a_v_s)
    state_b_prompt = jnp.stack(b_p)
    state_b_sample = jnp.stack(b_s)
    cache_k_prompt = jnp.stack(k_p)
    cache_v_prompt = jnp.stack(v_p)
    cache_k_sample = jnp.stack(k_s)
    cache_v_sample = jnp.stack(v_s)
    return (hp, hs, state_a_v_sample, state_b_prompt, state_b_sample, cache_k_prompt, cache_v_prompt, cache_k_sample, cache_v_sample)
```

```python
import functools

import jax
import jax.numpy as jnp
from jax import lax
from jax.experimental import pallas as pl
from jax.experimental.pallas import tpu as pltpu

F32 = jnp.float32
BF16 = jnp.bfloat16

D_MODEL = 1024
DEPTH = 2
PAGE_SIZE = 128
A_HEADS = 4
A_HEAD_DIM = 128
A_WIDTH = A_HEADS * A_HEAD_DIM
A_CHUNK = 128
B_HEADS = 4
B_HEAD_DIM = 128
B_WIDTH = B_HEADS * B_HEAD_DIM
C_HEADS = 16
C_HEAD_DIM = 64
C_WIDTH = C_HEADS * C_HEAD_DIM
FFN_HIDDEN = 2816
DN_ALPHA = (2 * DEPTH) ** 0.25
LN_EPS = 1e-5
RMS_EPS = 1e-6
HEAD_PAIR = 2 * C_HEAD_DIM
SUB = 16
VMEM_LIMIT = 56 * 1024 * 1024


def _cparams(sem):
    return pltpu.CompilerParams(dimension_semantics=sem, vmem_limit_bytes=VMEM_LIMIT)


def _sigmoid(x):
    return 1.0 / (1.0 + jnp.exp(-x))


def _silu(x):
    return x * _sigmoid(x)


def _gelu_tanh(x):
    return 0.5 * x * (1.0 + jnp.tanh(0.7978845608028654 * (x + 0.044715 * (x * x * x))))


def _layer_norm(x, g, b):
    mu = jnp.mean(x, axis=-1, keepdims=True)
    xc = x - mu
    var = jnp.mean(xc * xc, axis=-1, keepdims=True)
    return xc * lax.rsqrt(var + LN_EPS) * g + b


def _split3(x):
    hi = x.astype(BF16)
    r = x - hi.astype(F32)
    mid = r.astype(BF16)
    lo = (r - mid.astype(F32)).astype(BF16)
    return hi, mid, lo


def _dot(a, b):
    return jnp.dot(a, b, preferred_element_type=F32)


def _dot_nt(a, b):
    return lax.dot_general(a, b, (((1,), (1,)), ((), ())), preferred_element_type=F32)


def _dot_tn(a, b):
    return lax.dot_general(a, b, (((0,), (0,)), ((), ())), preferred_element_type=F32)


def _ab_in_kernel(h_ref, w_ref, lng_ref, lnb_ref, lbl_ref,
                  u_ref, vh_ref, q_ref, lf_ref, k_ref, vi_ref, g_ref):
    hb = h_ref[...].astype(BF16)

    def proj(j):
        return _dot(hb, w_ref[:, j * A_WIDTH:(j + 1) * A_WIDTH])

    u_ref[...] = _gelu_tanh(proj(0))
    v = _gelu_tanh(proj(1))
    for hd in range(A_HEADS):
        sl = slice(hd * A_HEAD_DIM, (hd + 1) * A_HEAD_DIM)
        vh_ref[:, sl] = _layer_norm(v[:, sl], lng_ref[:, sl], lnb_ref[:, sl])
    q_ref[...] = _silu(proj(2))
    lg = lbl_ref[...]
    ex = jnp.exp(lg - jnp.max(lg, axis=0, keepdims=True))
    lb = ex[0:1, :] / jnp.sum(ex, axis=0, keepdims=True)
    fp = proj(3)
    lf_ref[...] = jnp.log(lb + (1.0 - lb) * _sigmoid(fp))
    k_ref[...] = (1.0 - lb) * _sigmoid(-fp)
    vi_ref[...] = proj(4)
    g_ref[...] = _silu(proj(5))


def _ab_in(h, w_in, ln_g, ln_b, lb_logits):
    m = h.shape[0]
    tm = min(256, m)
    row = lambda i: (i, 0)
    fixed = lambda i: (0, 0)
    outs = [jax.ShapeDtypeStruct((m, A_WIDTH), F32)] * 7
    return pl.pallas_call(
        _ab_in_kernel,
        grid=(m // tm,),
        in_specs=[pl.BlockSpec((tm, D_MODEL), row),
                  pl.BlockSpec(w_in.shape, fixed),
                  pl.BlockSpec((1, A_WIDTH), fixed),
                  pl.BlockSpec((1, A_WIDTH), fixed),
                  pl.BlockSpec(lb_logits.shape, fixed)],
        out_specs=[pl.BlockSpec((tm, A_WIDTH), row)] * 7,
        out_shape=outs,
        compiler_params=_cparams(("parallel",)),
        name="ab_in",
    )(h, w_in, ln_g, ln_b, lb_logits)


def _hgrn_chunk(q, kk, vi, lf, st, ltri):
    c = q.shape[0]
    nsub = c // SUB
    hi, mid, lo = _split3(lf)
    g = _dot(ltri, hi) + _dot(ltri, mid) + _dot(ltri, lo)
    g_last = g[c - 1:c, :]
    o_inter = _dot_nt((q * jnp.exp(g)).astype(BF16), st.astype(BF16))
    kd = kk * jnp.exp(g_last - g)
    st_new = st * jnp.exp(g_last) + _dot_tn(vi.astype(BF16), kd.astype(BF16))
    o_blocks = [o_inter[i * SUB:(i + 1) * SUB] for i in range(nsub)]
    for j in range(nsub - 1):
        i0 = SUB * (j + 1)
        r = g[i0 - 1:i0, :]
        a = (q[i0:] * jnp.exp(g[i0:] - r)).astype(BF16)
        b = (kk[i0 - SUB:i0] * jnp.exp(r - g[i0 - SUB:i0])).astype(BF16)
        sc = _dot_nt(a, b).astype(BF16)
        res = _dot(sc, vi[i0 - SUB:i0].astype(BF16))
        for i in range(j + 1, nsub):
            o_blocks[i] = o_blocks[i] + res[(i - j - 1) * SUB:(i - j) * SUB]
    half = SUB // 2
    tid = lax.broadcasted_iota(jnp.int32, (half, q.shape[1]), 0)
    for jb in range(nsub):
        base = jb * SUB
        groups = []
        for rg in range(2):
            lo_r = base + rg * half
            gt = g[lo_r:lo_r + half]
            qt = q[lo_r:lo_r + half]
            acc = jnp.zeros_like(gt)
            for s in range(half * (rg + 1)):
                gs = g[base + s:base + s + 1]
                diff = gt - gs
                if s >= rg * half:
                    diff = jnp.where(tid >= s - rg * half, diff, -jnp.inf)
                p = qt * jnp.exp(diff) * kk[base + s:base + s + 1]
                cs = jnp.sum(p, axis=-1, keepdims=True)
                acc = acc + cs * vi[base + s:base + s + 1]
            groups.append(acc)
        o_blocks[jb] = o_blocks[jb] + jnp.concatenate(groups, axis=0)
    return jnp.concatenate(o_blocks, axis=0), st_new


def _hgrn_kernel(q_ref, k_ref, v_ref, lf_ref, gate_ref, s0_ref, ng_ref, o_ref, s_ref, st_ref,
                 *, chunk, nchunks):
    i = pl.program_id(2)

    @pl.when(i == 0)
    def _():
        st_ref[...] = s0_ref[0, 0].T

    row = lax.broadcasted_iota(jnp.int32, (chunk, chunk), 0)
    col = lax.broadcasted_iota(jnp.int32, (chunk, chunk), 1)
    ltri = (row >= col).astype(BF16)
    ng = ng_ref[...]

    def body(ci, carry):
        r0 = pl.multiple_of(ci * chunk, chunk)
        rows = pl.ds(r0, chunk)
        o, st_new = _hgrn_chunk(q_ref[0, rows, :], k_ref[0, rows, :], v_ref[0, rows, :],
                                lf_ref[0, rows, :], st_ref[...], ltri)
        st_ref[...] = st_new
        o = o * lax.rsqrt(jnp.mean(o * o, axis=-1, keepdims=True) + RMS_EPS) * ng
        o_ref[0, rows, :] = o * gate_ref[0, rows, :]
        return carry

    lax.fori_loop(0, nchunks, body, 0)

    @pl.when(i == pl.num_programs(2) - 1)
    def _():
        s_ref[0, 0] = st_ref[...].T


def _hgrn(q, kk, vi, lf, gate, s0, norm_g, chunk):
    n, l, _ = q.shape
    tb = min(l, 4 * chunk)
    seq = lambda b, h, i: (b, i, h)
    st = lambda b, h, i: (b, h, 0, 0)
    blk = pl.BlockSpec((1, tb, B_HEAD_DIM), seq)
    kern = functools.partial(_hgrn_kernel, chunk=chunk, nchunks=tb // chunk)
    return pl.pallas_call(
        kern,
        grid=(n, B_HEADS, l // tb),
        in_specs=[blk, blk, blk, blk, blk,
                  pl.BlockSpec((1, 1, B_HEAD_DIM, B_HEAD_DIM), st),
                  pl.BlockSpec((1, B_HEAD_DIM), lambda b, h, i: (0, 0))],
        out_specs=[blk, pl.BlockSpec((1, 1, B_HEAD_DIM, B_HEAD_DIM), st)],
        out_shape=[jax.ShapeDtypeStruct((n, l, B_WIDTH), F32),
                   jax.ShapeDtypeStruct((n, B_HEADS, B_HEAD_DIM, B_HEAD_DIM), F32)],
        scratch_shapes=[pltpu.VMEM((B_HEAD_DIM, B_HEAD_DIM), F32)],
        compiler_params=_cparams(("parallel", "parallel", "arbitrary")),
        name="hgrn2",
    )(q, kk, vi, lf, gate, s0, norm_g)


def _ab_out_kernel(x_ref, u_ref, vh_ref, ob_ref, ws_ref, bs_ref, wo_ref, g_ref, b_ref,
                   o_ref, a_ref):
    tm = x_ref.shape[0]
    row = lax.broadcasted_iota(jnp.int32, (A_CHUNK, A_CHUNK), 0)
    col = lax.broadcasted_iota(jnp.int32, (A_CHUNK, A_CHUNK), 1)
    causal = row >= col
    for hd in range(A_HEADS):
        w = jnp.where(causal, ws_ref[hd], 0.0).astype(BF16)
        cs = slice(hd * A_HEAD_DIM, (hd + 1) * A_HEAD_DIM)
        for c in range(tm // A_CHUNK):
            rs = slice(c * A_CHUNK, (c + 1) * A_CHUNK)
            sv = _dot(w, vh_ref[rs, cs].astype(BF16)) + bs_ref[hd]
            a_ref[rs, cs] = (u_ref[rs, cs] * sv).astype(BF16)
    y = _dot(a_ref[...], wo_ref[0:A_WIDTH, :]) + _dot(ob_ref[...].astype(BF16), wo_ref[A_WIDTH:, :])
    o_ref[...] = _layer_norm(DN_ALPHA * x_ref[...] + y, g_ref[...], b_ref[...])


def _ab_out(x, u, vh, ob, ws, bs, w_out, g, b):
    m = x.shape[0]
    tm = min(512, m)
    row = lambda i: (i, 0)
    fixed2 = lambda i: (0, 0)
    return pl.pallas_call(
        _ab_out_kernel,
        grid=(m // tm,),
        in_specs=[pl.BlockSpec((tm, D_MODEL), row),
                  pl.BlockSpec((tm, A_WIDTH), row),
                  pl.BlockSpec((tm, A_WIDTH), row),
                  pl.BlockSpec((tm, B_WIDTH), row),
                  pl.BlockSpec(ws.shape, lambda i: (0, 0, 0)),
                  pl.BlockSpec(bs.shape, lambda i: (0, 0, 0)),
                  pl.BlockSpec(w_out.shape, fixed2),
                  pl.BlockSpec((1, D_MODEL), fixed2),
                  pl.BlockSpec((1, D_MODEL), fixed2)],
        out_specs=pl.BlockSpec((tm, D_MODEL), row),
        out_shape=jax.ShapeDtypeStruct((m, D_MODEL), F32),
        scratch_shapes=[pltpu.VMEM((tm, A_WIDTH), BF16)],
        compiler_params=_cparams(("parallel",)),
        name="ab_out",
    )(x, u, vh, ob, ws, bs, w_out, g, b)


def _ffn_kernel(x_ref, wg_ref, wu_ref, wd_ref, g_ref, b_ref, o_ref, *, th):
    x = x_ref[...]
    xb = x.astype(BF16)
    y = jnp.zeros(x.shape, F32)
    for j in range(FFN_HIDDEN // th):
        cs = slice(j * th, (j + 1) * th)
        gt = _dot(xb, wg_ref[:, cs])
        up = _dot(xb, wu_ref[:, cs])
        y = y + _dot((_silu(gt) * up).astype(BF16), wd_ref[cs, :])
    o_ref[...] = _layer_norm(DN_ALPHA * x + y, g_ref[...], b_ref[...])


def _ffn(x, wg, wu, wd, g, b):
    m = x.shape[0]
    tm = min(512, m)
    row = lambda i: (i, 0)
    fixed = lambda i: (0, 0)
    once = pl.Buffered(1)
    return pl.pallas_call(
        functools.partial(_ffn_kernel, th=FFN_HIDDEN // 2),
        grid=(m // tm,),
        in_specs=[pl.BlockSpec((tm, D_MODEL), row),
                  pl.BlockSpec(wg.shape, fixed, pipeline_mode=once),
                  pl.BlockSpec(wu.shape, fixed, pipeline_mode=once),
                  pl.BlockSpec(wd.shape, fixed, pipeline_mode=once),
                  pl.BlockSpec((1, D_MODEL), fixed),
                  pl.BlockSpec((1, D_MODEL), fixed)],
        out_specs=pl.BlockSpec((tm, D_MODEL), row),
        out_shape=jax.ShapeDtypeStruct((m, D_MODEL), F32),
        compiler_params=_cparams(("parallel",)),
        name="ffn",
    )(x, wg, wu, wd, g, b)


def _qkv_rows_kernel(h_ref, w_ref, q_ref, k_ref, v_ref):
    hb = h_ref[...].astype(BF16)
    q_ref[...] = _dot(hb, w_ref[:, 0:C_WIDTH]) * (C_HEAD_DIM ** -0.5)
    k_ref[...] = _dot(hb, w_ref[:, C_WIDTH:2 * C_WIDTH])
    v_ref[...] = _dot(hb, w_ref[:, 2 * C_WIDTH:])


def _qkv_rows(h, w_in):
    m = h.shape[0]
    tm = min(512, m)
    row = lambda i: (i, 0)
    blk = pl.BlockSpec((tm, C_WIDTH), row)
    return pl.pallas_call(
        _qkv_rows_kernel,
        grid=(m // tm,),
        in_specs=[pl.BlockSpec((tm, D_MODEL), row), pl.BlockSpec(w_in.shape, lambda i: (0, 0))],
        out_specs=[blk] * 3,
        out_shape=[jax.ShapeDtypeStruct((m, C_WIDTH), F32)] * 3,
        compiler_params=_cparams(("parallel",)),
        name="qkv_rows",
    )(h, w_in)


def _qkv_cols_kernel(h_ref, wq_ref, wkt_ref, wvt_ref, qb_ref, kt_ref, vt_ref):
    hb = h_ref[0].astype(BF16)
    qb_ref[0] = (_dot(hb, wq_ref[...]) * (C_HEAD_DIM ** -0.5)).astype(BF16)
    kt_ref[0] = _dot_nt(wkt_ref[...], hb)
    vt_ref[0] = _dot_nt(wvt_ref[...], hb)


def _qkv_cols(h, wq, wkt, wvt):
    n, l, _ = h.shape
    tm = min(512, l)
    row = lambda b, i: (b, i, 0)
    colmap = lambda b, i: (b, 0, i)
    fixed = lambda b, i: (0, 0)
    wspec = pl.BlockSpec((C_WIDTH, D_MODEL), fixed)
    return pl.pallas_call(
        _qkv_cols_kernel,
        grid=(n, l // tm),
        in_specs=[pl.BlockSpec((1, tm, D_MODEL), row), pl.BlockSpec((D_MODEL, C_WIDTH), fixed),
                  wspec, wspec],
        out_specs=[pl.BlockSpec((1, tm, C_WIDTH), row),
                   pl.BlockSpec((1, C_WIDTH, tm), colmap),
                   pl.BlockSpec((1, C_WIDTH, tm), colmap)],
        out_shape=[jax.ShapeDtypeStruct((n, l, C_WIDTH), BF16),
                   jax.ShapeDtypeStruct((n, C_WIDTH, l), F32),
                   jax.ShapeDtypeStruct((n, C_WIDTH, l), F32)],
        compiler_params=_cparams(("parallel", "parallel")),
        name="qkv_cols",
    )(h, wq, wkt, wvt)


def _sb_tile(z, carry, upper, mask):
    l = jnp.log(1.0 + jnp.exp(-jnp.abs(z)))
    log_beta = jnp.minimum(z, 0.0) - l
    log_1m = log_beta - z
    if mask is not None:
        log_1m = jnp.where(mask, log_1m, 0.0)
    hi = log_1m.astype(BF16)
    lo = (log_1m - hi.astype(F32)).astype(BF16)
    rest = _dot(hi, upper) + _dot(lo, upper)
    w = jnp.exp(log_beta + rest + carry)
    if mask is not None:
        w = jnp.where(mask, w, 0.0)
    return w, jnp.sum(log_1m, axis=-1, keepdims=True)


def _sb_prompt_kernel(bias_ref, q_ref, k_ref, v_ref, o_ref, acc_ref, c_ref, *, t):
    hp = pl.program_id(1)
    qi = pl.program_id(2)
    q2 = q_ref[0]
    lane = lax.broadcasted_iota(jnp.int32, q2.shape, 1)
    first = lane < C_HEAD_DIM
    zero = jnp.zeros_like(q2)
    qh = (jnp.where(first, q2, zero), jnp.where(first, zero, q2))
    row = lax.broadcasted_iota(jnp.int32, (t, t), 0)
    col = lax.broadcasted_iota(jnp.int32, (t, t), 1)
    upper = (row > col).astype(BF16)
    causal = col < row
    acc_ref[...] = jnp.zeros_like(acc_ref)
    c_ref[...] = jnp.zeros_like(c_ref)

    def tile(kt, mask):
        ks = pl.ds(pl.multiple_of(kt * t, t), t)
        k2 = k_ref[0, :, ks].astype(BF16)
        v2 = v_ref[0, :, ks].astype(BF16)
        for h in range(2):
            z = _dot(qh[h], k2) + bias_ref[2 * hp + h]
            w, tot = _sb_tile(z, c_ref[h], upper, mask)
            acc_ref[h] += _dot_nt(w.astype(BF16), v2)
            c_ref[h] += tot

    tile(qi, causal)

    def body(j, carry):
        tile(qi - 1 - j, None)
        return carry

    lax.fori_loop(0, qi, body, 0)
    o_ref[0] = jnp.where(first, acc_ref[0], acc_ref[1]).astype(o_ref.dtype)


def _sb_prompt(qb, kt, vt, bias):
    n, l, _ = qb.shape
    t = min(256, l)
    qmap = lambda b, hp, i: (b, i, hp)
    kmap = lambda b, hp, i: (b, hp, 0)
    out = pl.pallas_call(
        functools.partial(_sb_prompt_kernel, t=t),
        grid=(n, C_HEADS // 2, l // t),
        in_specs=[pl.BlockSpec(memory_space=pltpu.SMEM),
                  pl.BlockSpec((1, t, HEAD_PAIR), qmap),
                  pl.BlockSpec((1, HEAD_PAIR, l), kmap),
                  pl.BlockSpec((1, HEAD_PAIR, l), kmap)],
        out_specs=pl.BlockSpec((1, t, HEAD_PAIR), qmap),
        out_shape=jax.ShapeDtypeStruct((n, l, C_WIDTH), BF16),
        scratch_shapes=[pltpu.VMEM((2, t, HEAD_PAIR), F32), pltpu.VMEM((2, t, 1), F32)],
        compiler_params=_cparams(("parallel", "parallel", "parallel")),
        name="sb_prompt",
    )(bias, qb, kt, vt)
    return out.reshape(n * l, C_WIDTH)


def _sb_sample_kernel(pt_ref, q_ref, kn_ref, vn_ref, kp_ref, vp_ref, bias_ref, o_ref,
                      qbd_ref, acc_ref, c_ref, *, lq):
    del pt_ref
    p = pl.program_id(1)
    rows = lq * C_HEADS
    r16 = lax.broadcasted_iota(jnp.int32, (C_HEADS, C_WIDTH), 0)
    c16 = lax.broadcasted_iota(jnp.int32, (C_HEADS, C_WIDTH), 1)
    own = r16 == c16 // C_HEAD_DIM
    row = lax.broadcasted_iota(jnp.int32, (PAGE_SIZE, PAGE_SIZE), 0)
    col = lax.broadcasted_iota(jnp.int32, (PAGE_SIZE, PAGE_SIZE), 1)
    upper = (row > col).astype(BF16)
    bias = bias_ref[...]

    def tile(k2, v2, mask):
        z = _dot(qbd_ref[...], k2) + bias
        w, tot = _sb_tile(z, c_ref[...], upper, mask)
        acc_ref[...] += _dot_nt(w.astype(BF16), v2)
        c_ref[...] += tot

    @pl.when(p == 0)
    def _():
        q = q_ref[0]
        for tq in range(lq):
            blk = jnp.broadcast_to(q[tq:tq + 1, :], (C_HEADS, C_WIDTH))
            qbd_ref[tq * C_HEADS:(tq + 1) * C_HEADS, :] = jnp.where(own, blk, 0.0).astype(BF16)
        acc_ref[...] = jnp.zeros_like(acc_ref)
        c_ref[...] = jnp.zeros_like(c_ref)
        qrow = lax.broadcasted_iota(jnp.int32, (rows, PAGE_SIZE), 0) // C_HEADS
        kcol = lax.broadcasted_iota(jnp.int32, (rows, PAGE_SIZE), 1)
        tile(kn_ref[0], vn_ref[0], kcol < qrow)

    @pl.when(p > 0)
    def _():
        tile(kp_ref[0].astype(BF16), vp_ref[0].astype(BF16), None)

    @pl.when(p == pl.num_programs(1) - 1)
    def _():
        for tq in range(lq):
            blk = acc_ref[tq * C_HEADS:(tq + 1) * C_HEADS, :]
            o_ref[0, tq:tq + 1, :] = jnp.sum(jnp.where(own, blk, 0.0), axis=0,
                                             keepdims=True).astype(o_ref.dtype)


def _sb_sample(q, k, v, pool_k, pool_v, page_table, bias, n, lq):
    n_pages = page_table.shape[1]
    n_pool = pool_k.shape[0]
    rows = lq * C_HEADS
    q3 = q.reshape(n, lq, C_WIDTH)
    pad = ((0, 0), (0, 0), (0, PAGE_SIZE - lq))
    cols = lambda t: jnp.pad(t.reshape(n, lq, C_WIDTH).transpose(0, 2, 1).astype(BF16), pad)
    kn = cols(k)
    vn = cols(v)
    pages = lambda t: t.transpose(0, 2, 3, 1).reshape(n_pool, C_WIDTH, PAGE_SIZE)
    kp = pages(pool_k)
    vp = pages(pool_v)
    bias_col = jnp.tile(bias.astype(F32), lq).reshape(rows, 1)
    seq = lambda b, p, pt: (b, 0, 0)
    page = lambda b, p, pt: (pt[b, jnp.minimum(n_pages - p, n_pages - 1)], 0, 0)
    grid_spec = pltpu.PrefetchScalarGridSpec(
        num_scalar_prefetch=1,
        grid=(n, n_pages + 1),
        in_specs=[pl.BlockSpec((1, lq, C_WIDTH), seq),
                  pl.BlockSpec((1, C_WIDTH, PAGE_SIZE), seq),
                  pl.BlockSpec((1, C_WIDTH, PAGE_SIZE), seq),
                  pl.BlockSpec((1, C_WIDTH, PAGE_SIZE), page),
                  pl.BlockSpec((1, C_WIDTH, PAGE_SIZE), page),
                  pl.BlockSpec((rows, 1), lambda b, p, pt: (0, 0))],
        out_specs=pl.BlockSpec((1, lq, C_WIDTH), seq),
        scratch_shapes=[pltpu.VMEM((rows, C_WIDTH), BF16),
                        pltpu.VMEM((rows, C_WIDTH), F32),
                        pltpu.VMEM((rows, 1), F32)],
    )
    out = pl.pallas_call(
        functools.partial(_sb_sample_kernel, lq=lq),
        grid_spec=grid_spec,
        out_shape=jax.ShapeDtypeStruct((n, lq, C_WIDTH), F32),
        compiler_params=_cparams(("parallel", "arbitrary")),
        name="sb_sample",
    )(page_table, q3, kn, vn, kp, vp, bias_col)
    return out.reshape(n * lq, C_WIDTH).astype(BF16)


def _c_out_kernel(x_ref, a_ref, wo_ref, g_ref, b_ref, o_ref):
    y = _dot(a_ref[...], wo_ref[...])
    o_ref[...] = _layer_norm(DN_ALPHA * x_ref[...] + y, g_ref[...], b_ref[...])


def _c_out(x, a, w_out, g, b):
    m = x.shape[0]
    tm = min(512, m)
    row = lambda i: (i, 0)
    fixed = lambda i: (0, 0)
    return pl.pallas_call(
        _c_out_kernel,
        grid=(m // tm,),
        in_specs=[pl.BlockSpec((tm, D_MODEL), row),
                  pl.BlockSpec((tm, C_WIDTH), row),
                  pl.BlockSpec(w_out.shape, fixed),
                  pl.BlockSpec((1, D_MODEL), fixed),
                  pl.BlockSpec((1, D_MODEL), fixed)],
        out_specs=pl.BlockSpec((tm, D_MODEL), row),
        out_shape=jax.ShapeDtypeStruct((m, D_MODEL), F32),
        compiler_params=_cparams(("parallel",)),
        name="c_out",
    )(x, a, w_out, g, b)


def _mixer_ab(hp, hs, np_, lp, ns, ls, state_b, w_in, w_out, a_w_s, a_b_s, a_ln_g, a_ln_b,
              lb_logits, norm_g, ln_g, ln_b):
    w_in = w_in.astype(BF16)
    w_out = w_out.astype(BF16)
    a_ln_g = a_ln_g.reshape(1, A_WIDTH)
    a_ln_b = a_ln_b.reshape(1, A_WIDTH)
    norm_g = norm_g.reshape(1, B_HEAD_DIM)
    ln_g = ln_g.reshape(1, D_MODEL)
    ln_b = ln_b.reshape(1, D_MODEL)

    u, vh, q, lf, kk, vi, gate = _ab_in(hp, w_in, a_ln_g, a_ln_b, lb_logits)
    seq = lambda t: t.reshape(np_, lp, B_WIDTH)
    s0 = jnp.zeros((np_, B_HEADS, B_HEAD_DIM, B_HEAD_DIM), F32)
    ob, sp = _hgrn(seq(q), seq(kk), seq(vi), seq(lf), seq(gate), s0, norm_g, chunk=128)
    lanes = lambda bcol: jnp.broadcast_to(bcol[:, :, None], bcol.shape + (A_HEAD_DIM,))
    xp = _ab_out(hp, u, vh, ob.reshape(np_ * lp, B_WIDTH), a_w_s, lanes(a_b_s), w_out, ln_g, ln_b)

    u, vh, q, lf, kk, vi, gate = _ab_in(hs, w_in, a_ln_g, a_ln_b, lb_logits)
    pad = lambda t: jnp.pad(t.reshape(ns, ls, B_WIDTH), ((0, 0), (0, SUB - ls), (0, 0)))
    ob, ss = _hgrn(pad(q), pad(kk), pad(vi), pad(lf), pad(gate), state_b, norm_g, chunk=SUB)
    ob = ob[:, :ls].reshape(ns * ls, B_WIDTH)
    eye = jnp.eye(ns, dtype=F32)
    ws_blk = jnp.einsum("ab,hts->hatbs", eye, a_w_s[:, :ls, :ls]).reshape(A_HEADS, ns * ls, ns * ls)
    bs_blk = jnp.tile(a_b_s[:, :ls], (1, ns))
    xs = _ab_out(hs, u, vh, ob, ws_blk, lanes(bs_blk), w_out, ln_g, ln_b)
    return xp, xs, vh.reshape(ns, ls, A_WIDTH), sp, ss


def _mixer_c(hp, hs, np_, lp, ns, ls, pool_k, pool_v, page_table, w_in, w_out, bias, ln_g, ln_b):
    w_in = w_in.astype(BF16)
    w_out = w_out.astype(BF16)
    ln_g = ln_g.reshape(1, D_MODEL)
    ln_b = ln_b.reshape(1, D_MODEL)
    bias = bias.astype(F32)
    wq = w_in[:, :C_WIDTH]
    wkt = w_in[:, C_WIDTH:2 * C_WIDTH].T
    wvt = w_in[:, 2 * C_WIDTH:].T
    qb, kt, vt = _qkv_cols(hp.reshape(np_, lp, D_MODEL), wq, wkt, wvt)
    ap = _sb_prompt(qb, kt, vt, bias)
    xp = _c_out(hp, ap, w_out, ln_g, ln_b)
    q, kn, vn = _qkv_rows(hs, w_in)
    a_s = _sb_sample(q, kn, vn, pool_k, pool_v, page_table, bias, ns, ls)
    xs = _c_out(hs, a_s, w_out, ln_g, ln_b)
    heads = lambda t: t.reshape(ns, ls, C_HEADS, C_HEAD_DIM)
    pos_major = lambda t: t.reshape(np_, C_HEADS, C_HEAD_DIM, lp).transpose(0, 3, 1, 2)
    return xp, xs, pos_major(kt), pos_major(vt), heads(kn), heads(vn)


def kernel(x_prompt, x_sample, state_b, cache_k, cache_v, page_table, ab_w_in, ab_w_out, a_w_s, a_b_s, a_ln_g, a_ln_b, b_lb_logits, b_norm_g, c_w_in, c_w_out, c_logit_bias, ln1_g, ln1_b, ln2_g, ln2_b, ffn_w_gate, ffn_w_up, ffn_w_down):
    np_, lp, _ = x_prompt.shape
    ns, ls, _ = x_sample.shape
    hp = x_prompt.reshape(np_ * lp, D_MODEL)
    hs = x_sample.reshape(ns * ls, D_MODEL)
    a_v_s, b_p, b_s, k_p, v_p, k_s, v_s = [], [], [], [], [], [], []
    for layer in range(DEPTH):
        i = layer // 2
        if layer % 2 == 0:
            hp, hs, av, sp, ss = _mixer_ab(
                hp, hs, np_, lp, ns, ls, state_b[i], ab_w_in[i], ab_w_out[i], a_w_s[i], a_b_s[i],
                a_ln_g[i], a_ln_b[i], b_lb_logits.astype(F32), b_norm_g[i], ln1_g[layer], ln1_b[layer])
            a_v_s.append(av)
            b_p.append(sp)
            b_s.append(ss)
        else:
            hp, hs, kp, vp, kn, vn = _mixer_c(
                hp, hs, np_, lp, ns, ls, cache_k[i], cache_v[i], page_table, c_w_in[i], c_w_out[i],
                c_logit_bias[i], ln1_g[layer], ln1_b[layer])
            k_p.append(kp)
            v_p.append(vp)
            k_s.append(kn)
            v_s.append(vn)
        wg = ffn_w_gate[layer].astype(BF16)
        wu = ffn_w_up[layer].astype(BF16)
        wd = ffn_w_down[layer].astype(BF16)
        g2 = ln2_g[layer].reshape(1, D_MODEL)
        b2 = ln2_b[layer].reshape(1, D_MODEL)
        hp = _ffn(hp, wg, wu, wd, g2, b2)
        hs = _ffn(hs, wg, wu, wd, g2, b2)
    return (hp.reshape(np_, lp, D_MODEL), hs.reshape(ns, ls, D_MODEL), jnp.stack(a_v_s),
            jnp.stack(b_p), jnp.stack(b_s), jnp.stack(k_p), jnp.stack(v_p), jnp.stack(k_s),
            jnp.stack(v_s))
```

```python
import functools

import jax
import jax.numpy as jnp
from jax import lax
from jax.experimental import pallas as pl
from jax.experimental.pallas import tpu as pltpu

F32 = jnp.float32
BF16 = jnp.bfloat16

D_MODEL = 1024
DEPTH = 2
PAGE_SIZE = 128
A_HEADS = 4
A_HEAD_DIM = 128
A_WIDTH = A_HEADS * A_HEAD_DIM
A_CHUNK = 128
B_HEADS = 4
B_HEAD_DIM = 128
B_WIDTH = B_HEADS * B_HEAD_DIM
C_HEADS = 16
C_HEAD_DIM = 64
C_WIDTH = C_HEADS * C_HEAD_DIM
FFN_HIDDEN = 2816
DN_ALPHA = (2 * DEPTH) ** 0.25
LN_EPS = 1e-5
RMS_EPS = 1e-6
LOG2E = 1.4426950408889634
Q_SCALE = C_HEAD_DIM ** -0.5 * LOG2E
HEAD_PAIR = 2 * C_HEAD_DIM
SUB = 16
VMEM_LIMIT = 56 * 1024 * 1024


def _cparams(sem):
    return pltpu.CompilerParams(dimension_semantics=sem, vmem_limit_bytes=VMEM_LIMIT)


def _sigmoid(x):
    return 1.0 / (1.0 + jnp.exp(-x))


def _silu(x):
    return x * _sigmoid(x)


def _gelu_tanh(x):
    return 0.5 * x * (1.0 + jnp.tanh(0.7978845608028654 * (x + 0.044715 * (x * x * x))))


def _layer_norm(x, g, b):
    mu = jnp.mean(x, axis=-1, keepdims=True)
    xc = x - mu
    var = jnp.mean(xc * xc, axis=-1, keepdims=True)
    return xc * lax.rsqrt(var + LN_EPS) * g + b


def _split3(x):
    hi = x.astype(BF16)
    r = x - hi.astype(F32)
    mid = r.astype(BF16)
    lo = (r - mid.astype(F32)).astype(BF16)
    return hi, mid, lo


def _dot(a, b):
    return jnp.dot(a, b, preferred_element_type=F32)


def _dot_nt(a, b):
    return lax.dot_general(a, b, (((1,), (1,)), ((), ())), preferred_element_type=F32)


def _dot_tn(a, b):
    return lax.dot_general(a, b, (((0,), (0,)), ((), ())), preferred_element_type=F32)


def _ab_in_kernel(h_ref, w_ref, lng_ref, lnb_ref, lbl_ref,
                  u_ref, vh_ref, q_ref, lf_ref, k_ref, vi_ref, g_ref):
    hb = h_ref[...].astype(BF16)

    def proj(j):
        return _dot(hb, w_ref[:, j * A_WIDTH:(j + 1) * A_WIDTH])

    u_ref[...] = _gelu_tanh(proj(0))
    v = _gelu_tanh(proj(1))
    for hd in range(A_HEADS):
        sl = slice(hd * A_HEAD_DIM, (hd + 1) * A_HEAD_DIM)
        vh_ref[:, sl] = _layer_norm(v[:, sl], lng_ref[:, sl], lnb_ref[:, sl])
    q_ref[...] = _silu(proj(2))
    lg = lbl_ref[...]
    ex = jnp.exp(lg - jnp.max(lg, axis=0, keepdims=True))
    lb = ex[0:1, :] / jnp.sum(ex, axis=0, keepdims=True)
    fp = proj(3)
    lf_ref[...] = jnp.log(lb + (1.0 - lb) * _sigmoid(fp))
    k_ref[...] = (1.0 - lb) * _sigmoid(-fp)
    vi_ref[...] = proj(4)
    g_ref[...] = _silu(proj(5))


def _ab_in(h, w_in, ln_g, ln_b, lb_logits):
    m = h.shape[0]
    tm = min(256, m)
    row = lambda i: (i, 0)
    fixed = lambda i: (0, 0)
    outs = [jax.ShapeDtypeStruct((m, A_WIDTH), F32)] * 7
    return pl.pallas_call(
        _ab_in_kernel,
        grid=(m // tm,),
        in_specs=[pl.BlockSpec((tm, D_MODEL), row),
                  pl.BlockSpec(w_in.shape, fixed),
                  pl.BlockSpec((1, A_WIDTH), fixed),
                  pl.BlockSpec((1, A_WIDTH), fixed),
                  pl.BlockSpec(lb_logits.shape, fixed)],
        out_specs=[pl.BlockSpec((tm, A_WIDTH), row)] * 7,
        out_shape=outs,
        compiler_params=_cparams(("parallel",)),
        name="ab_in",
    )(h, w_in, ln_g, ln_b, lb_logits)


def _hgrn_chunk(q, kk, vi, lf, st, ltri):
    c = q.shape[0]
    nsub = c // SUB
    hi, mid, lo = _split3(lf)
    g = _dot(ltri, hi) + _dot(ltri, mid) + _dot(ltri, lo)
    g_last = g[c - 1:c, :]
    o_inter = _dot_nt((q * jnp.exp(g)).astype(BF16), st.astype(BF16))
    kd = kk * jnp.exp(g_last - g)
    st_new = st * jnp.exp(g_last) + _dot_tn(vi.astype(BF16), kd.astype(BF16))
    o_blocks = [o_inter[i * SUB:(i + 1) * SUB] for i in range(nsub)]
    for j in range(nsub - 1):
        i0 = SUB * (j + 1)
        r = g[i0 - 1:i0, :]
        a = (q[i0:] * jnp.exp(g[i0:] - r)).astype(BF16)
        b = (kk[i0 - SUB:i0] * jnp.exp(r - g[i0 - SUB:i0])).astype(BF16)
        sc = _dot_nt(a, b).astype(BF16)
        res = _dot(sc, vi[i0 - SUB:i0].astype(BF16))
        for i in range(j + 1, nsub):
            o_blocks[i] = o_blocks[i] + res[(i - j - 1) * SUB:(i - j) * SUB]
    half = SUB // 2
    tid = lax.broadcasted_iota(jnp.int32, (half, q.shape[1]), 0)
    for jb in range(nsub):
        base = jb * SUB
        groups = []
        for rg in range(2):
            lo_r = base + rg * half
            gt = g[lo_r:lo_r + half]
            qt = q[lo_r:lo_r + half]
            acc = jnp.zeros_like(gt)
            for s in range(half * (rg + 1)):
                gs = g[base + s:base + s + 1]
                diff = gt - gs
                if s >= rg * half:
                    diff = jnp.where(tid >= s - rg * half, diff, -jnp.inf)
                p = qt * jnp.exp(diff) * kk[base + s:base + s + 1]
                cs = jnp.sum(p, axis=-1, keepdims=True)
                acc = acc + cs * vi[base + s:base + s + 1]
            groups.append(acc)
        o_blocks[jb] = o_blocks[jb] + jnp.concatenate(groups, axis=0)
    return jnp.concatenate(o_blocks, axis=0), st_new


def _hgrn_kernel(q_ref, k_ref, v_ref, lf_ref, gate_ref, s0_ref, ng_ref, o_ref, s_ref, st_ref,
                 *, chunk, nchunks):
    i = pl.program_id(2)

    @pl.when(i == 0)
    def _():
        st_ref[...] = s0_ref[0, 0].T

    row = lax.broadcasted_iota(jnp.int32, (chunk, chunk), 0)
    col = lax.broadcasted_iota(jnp.int32, (chunk, chunk), 1)
    ltri = (row >= col).astype(BF16)
    ng = ng_ref[...]

    def body(ci, carry):
        r0 = pl.multiple_of(ci * chunk, chunk)
        rows = pl.ds(r0, chunk)
        o, st_new = _hgrn_chunk(q_ref[0, rows, :], k_ref[0, rows, :], v_ref[0, rows, :],
                                lf_ref[0, rows, :], st_ref[...], ltri)
        st_ref[...] = st_new
        o = o * lax.rsqrt(jnp.mean(o * o, axis=-1, keepdims=True) + RMS_EPS) * ng
        o_ref[0, rows, :] = o * gate_ref[0, rows, :]
        return carry

    lax.fori_loop(0, nchunks, body, 0)

    @pl.when(i == pl.num_programs(2) - 1)
    def _():
        s_ref[0, 0] = st_ref[...].T


def _hgrn(q, kk, vi, lf, gate, s0, norm_g, chunk):
    n, l, _ = q.shape
    tb = min(l, 4 * chunk)
    seq = lambda b, h, i: (b, i, h)
    st = lambda b, h, i: (b, h, 0, 0)
    blk = pl.BlockSpec((1, tb, B_HEAD_DIM), seq)
    kern = functools.partial(_hgrn_kernel, chunk=chunk, nchunks=tb // chunk)
    return pl.pallas_call(
        kern,
        grid=(n, B_HEADS, l // tb),
        in_specs=[blk, blk, blk, blk, blk,
                  pl.BlockSpec((1, 1, B_HEAD_DIM, B_HEAD_DIM), st),
                  pl.BlockSpec((1, B_HEAD_DIM), lambda b, h, i: (0, 0))],
        out_specs=[blk, pl.BlockSpec((1, 1, B_HEAD_DIM, B_HEAD_DIM), st)],
        out_shape=[jax.ShapeDtypeStruct((n, l, B_WIDTH), F32),
                   jax.ShapeDtypeStruct((n, B_HEADS, B_HEAD_DIM, B_HEAD_DIM), F32)],
        scratch_shapes=[pltpu.VMEM((B_HEAD_DIM, B_HEAD_DIM), F32)],
        compiler_params=_cparams(("parallel", "parallel", "arbitrary")),
        name="hgrn2",
    )(q, kk, vi, lf, gate, s0, norm_g)


def _ab_out_kernel(x_ref, u_ref, vh_ref, ob_ref, ws_ref, bs_ref, wo_ref, g_ref, b_ref,
                   o_ref, a_ref):
    tm = x_ref.shape[0]
    row = lax.broadcasted_iota(jnp.int32, (A_CHUNK, A_CHUNK), 0)
    col = lax.broadcasted_iota(jnp.int32, (A_CHUNK, A_CHUNK), 1)
    causal = row >= col
    for hd in range(A_HEADS):
        w = jnp.where(causal, ws_ref[hd], 0.0).astype(BF16)
        cs = slice(hd * A_HEAD_DIM, (hd + 1) * A_HEAD_DIM)
        for c in range(tm // A_CHUNK):
            rs = slice(c * A_CHUNK, (c + 1) * A_CHUNK)
            sv = _dot(w, vh_ref[rs, cs].astype(BF16)) + bs_ref[hd]
            a_ref[rs, cs] = (u_ref[rs, cs] * sv).astype(BF16)
    y = _dot(a_ref[...], wo_ref[0:A_WIDTH, :]) + _dot(ob_ref[...].astype(BF16), wo_ref[A_WIDTH:, :])
    o_ref[...] = _layer_norm(DN_ALPHA * x_ref[...] + y, g_ref[...], b_ref[...])


def _ab_out(x, u, vh, ob, ws, bs, w_out, g, b):
    m = x.shape[0]
    tm = min(512, m)
    row = lambda i: (i, 0)
    fixed2 = lambda i: (0, 0)
    return pl.pallas_call(
        _ab_out_kernel,
        grid=(m // tm,),
        in_specs=[pl.BlockSpec((tm, D_MODEL), row),
                  pl.BlockSpec((tm, A_WIDTH), row),
                  pl.BlockSpec((tm, A_WIDTH), row),
                  pl.BlockSpec((tm, B_WIDTH), row),
                  pl.BlockSpec(ws.shape, lambda i: (0, 0, 0)),
                  pl.BlockSpec(bs.shape, lambda i: (0, 0, 0)),
                  pl.BlockSpec(w_out.shape, fixed2),
                  pl.BlockSpec((1, D_MODEL), fixed2),
                  pl.BlockSpec((1, D_MODEL), fixed2)],
        out_specs=pl.BlockSpec((tm, D_MODEL), row),
        out_shape=jax.ShapeDtypeStruct((m, D_MODEL), F32),
        scratch_shapes=[pltpu.VMEM((tm, A_WIDTH), BF16)],
        compiler_params=_cparams(("parallel",)),
        name="ab_out",
    )(x, u, vh, ob, ws, bs, w_out, g, b)


def _ffn_kernel(x_ref, wg_ref, wu_ref, wd_ref, g_ref, b_ref, o_ref, *, th):
    x = x_ref[...]
    xb = x.astype(BF16)
    y = jnp.zeros(x.shape, F32)
    for j in range(FFN_HIDDEN // th):
        cs = slice(j * th, (j + 1) * th)
        gt = _dot(xb, wg_ref[:, cs])
        up = _dot(xb, wu_ref[:, cs])
        y = y + _dot((_silu(gt) * up).astype(BF16), wd_ref[cs, :])
    o_ref[...] = _layer_norm(DN_ALPHA * x + y, g_ref[...], b_ref[...])


def _ffn(x, wg, wu, wd, g, b):
    m = x.shape[0]
    tm = min(512, m)
    row = lambda i: (i, 0)
    fixed = lambda i: (0, 0)
    once = pl.Buffered(1)
    return pl.pallas_call(
        functools.partial(_ffn_kernel, th=FFN_HIDDEN // 2),
        grid=(m // tm,),
        in_specs=[pl.BlockSpec((tm, D_MODEL), row),
                  pl.BlockSpec(wg.shape, fixed, pipeline_mode=once),
                  pl.BlockSpec(wu.shape, fixed, pipeline_mode=once),
                  pl.BlockSpec(wd.shape, fixed, pipeline_mode=once),
                  pl.BlockSpec((1, D_MODEL), fixed),
                  pl.BlockSpec((1, D_MODEL), fixed)],
        out_specs=pl.BlockSpec((tm, D_MODEL), row),
        out_shape=jax.ShapeDtypeStruct((m, D_MODEL), F32),
        compiler_params=_cparams(("parallel",)),
        name="ffn",
    )(x, wg, wu, wd, g, b)


def _qkv_rows_kernel(h_ref, w_ref, q_ref, k_ref, v_ref):
    hb = h_ref[...].astype(BF16)
    q_ref[...] = _dot(hb, w_ref[:, 0:C_WIDTH]) * Q_SCALE
    k_ref[...] = _dot(hb, w_ref[:, C_WIDTH:2 * C_WIDTH])
    v_ref[...] = _dot(hb, w_ref[:, 2 * C_WIDTH:])


def _qkv_rows(h, w_in):
    m = h.shape[0]
    tm = min(512, m)
    row = lambda i: (i, 0)
    blk = pl.BlockSpec((tm, C_WIDTH), row)
    return pl.pallas_call(
        _qkv_rows_kernel,
        grid=(m // tm,),
        in_specs=[pl.BlockSpec((tm, D_MODEL), row), pl.BlockSpec(w_in.shape, lambda i: (0, 0))],
        out_specs=[blk] * 3,
        out_shape=[jax.ShapeDtypeStruct((m, C_WIDTH), F32)] * 3,
        compiler_params=_cparams(("parallel",)),
        name="qkv_rows",
    )(h, w_in)


def _qkv_cols_kernel(h_ref, wq_ref, wkt_ref, wvt_ref, qb_ref, kt_ref, vt_ref):
    hb = h_ref[0].astype(BF16)
    qb_ref[0] = (_dot(hb, wq_ref[...]) * Q_SCALE).astype(BF16)
    kt_ref[0] = _dot_nt(wkt_ref[...], hb)
    vt_ref[0] = _dot_nt(wvt_ref[...], hb)


def _qkv_cols(h, wq, wkt, wvt):
    n, l, _ = h.shape
    tm = min(512, l)
    row = lambda b, i: (b, i, 0)
    colmap = lambda b, i: (b, 0, i)
    fixed = lambda b, i: (0, 0)
    wspec = pl.BlockSpec((C_WIDTH, D_MODEL), fixed)
    return pl.pallas_call(
        _qkv_cols_kernel,
        grid=(n, l // tm),
        in_specs=[pl.BlockSpec((1, tm, D_MODEL), row), pl.BlockSpec((D_MODEL, C_WIDTH), fixed),
                  wspec, wspec],
        out_specs=[pl.BlockSpec((1, tm, C_WIDTH), row),
                   pl.BlockSpec((1, C_WIDTH, tm), colmap),
                   pl.BlockSpec((1, C_WIDTH, tm), colmap)],
        out_shape=[jax.ShapeDtypeStruct((n, l, C_WIDTH), BF16),
                   jax.ShapeDtypeStruct((n, C_WIDTH, l), F32),
                   jax.ShapeDtypeStruct((n, C_WIDTH, l), F32)],
        compiler_params=_cparams(("parallel", "parallel")),
        name="qkv_cols",
    )(h, wq, wkt, wvt)


def _sb_gate_stage(z, upper2, mask):
    sp = jnp.log(1.0 + jnp.exp2(-jnp.abs(z))) * LOG2E
    log_beta = jnp.minimum(z, 0.0) - sp
    log_1m = log_beta - z
    if mask is not None:
        log_1m = jnp.where(mask, log_1m, 0.0)
    hi = log_1m.astype(BF16)
    lo = (log_1m - hi.astype(F32)).astype(BF16)
    rest = _dot(jnp.concatenate([hi, lo], axis=1), upper2)
    return log_beta, rest, jnp.sum(log_1m, axis=-1, keepdims=True)


def _sb_weight_stage(log_beta, rest, carry, mask):
    w = jnp.exp2(log_beta + rest + carry)
    if mask is not None:
        w = jnp.where(mask, w, 0.0)
    return w.astype(BF16)


def _newer_keys_matrix(tk):
    row = lax.broadcasted_iota(jnp.int32, (tk, tk), 0)
    col = lax.broadcasted_iota(jnp.int32, (tk, tk), 1)
    upper = (row > col).astype(BF16)
    return jnp.concatenate([upper, upper], axis=0)


def _sb_prompt_kernel(bias_ref, q_ref, k_ref, v_ref, o_ref, qm_ref, z_ref, acc_ref, c_ref,
                      *, tq, tk, rc):
    hp = pl.program_id(1)
    qi = pl.program_id(2)
    nd = tq // tk
    chunks = [(r, h) for r in range(tq // rc) for h in range(2)]
    lanes = tk // HEAD_PAIR
    q2 = q_ref[0]
    first = lax.broadcasted_iota(jnp.int32, q2.shape, 1) < C_HEAD_DIM
    zero = jnp.zeros_like(q2)
    qm_ref[0] = jnp.where(first, q2, zero)
    qm_ref[1] = jnp.where(first, zero, q2)
    upper2 = _newer_keys_matrix(tk)
    ahead = (lax.broadcasted_iota(jnp.int32, (rc, tk), 1)
             - lax.broadcasted_iota(jnp.int32, (rc, tk), 0))
    bias2 = [bias_ref[2 * hp + h] * LOG2E for h in range(2)]
    acc_ref[...] = jnp.zeros_like(acc_ref)
    c_ref[...] = jnp.zeros_like(c_ref)

    def key_cols(kt):
        return pl.ds(pl.multiple_of(kt * tk, tk), tk)

    def keys(kt):
        return k_ref[0, :, key_cols(kt)].astype(BF16)

    def logits(k2, r, h):
        rows = slice(r * rc, (r + 1) * rc)
        z_ref[h, rows, :] = _dot(qm_ref[h, rows, :], k2) + bias2[h]

    def tile(kt, row_off):
        v2 = v_ref[0, :, key_cols(kt)].astype(BF16)
        k2_next = keys(jnp.maximum(kt - 1, 0))
        staged = []

        def gate(r, h):
            rows = slice(r * rc, (r + 1) * rc)
            mask = None if row_off is None else ahead < row_off + r * rc
            staged.append((mask,) + _sb_gate_stage(z_ref[h, rows, :], upper2, mask))

        def weigh(idx):
            r, h = chunks[idx]
            mask, log_beta, rest, tot = staged[idx]
            rows = slice(r * rc, (r + 1) * rc)
            c = c_ref[h, rows, :]
            w = _sb_weight_stage(log_beta, rest, jnp.tile(c, (1, lanes)), mask)
            acc_ref[h, rows, :] += _dot_nt(w, v2)
            c_ref[h, rows, :] = c + tot

        lag = 2
        for idx, (r, h) in enumerate(chunks):
            gate(r, h)
            logits(k2_next, r, h)
            if idx >= lag:
                weigh(idx - lag)
        for idx in range(len(chunks) - lag, len(chunks)):
            weigh(idx)

    def own(d, carry):
        kt = (qi + 1) * nd - 1 - d
        tile(kt, qi * tq - kt * tk)
        return carry

    def older(j, carry):
        tile(qi * nd - 1 - j, None)
        return carry

    k2_first = keys((qi + 1) * nd - 1)
    for r, h in chunks:
        logits(k2_first, r, h)
    lax.fori_loop(0, nd, own, 0)
    lax.fori_loop(0, qi * nd, older, 0)
    o_ref[0] = jnp.where(first, acc_ref[0], acc_ref[1]).astype(o_ref.dtype)


SB_QUERY_TILE = 512
SB_KEY_TILE = 256
SB_ROW_CHUNK = 128


def _sb_prompt(qb, kt, vt, bias):
    n, l, _ = qb.shape
    tq = min(SB_QUERY_TILE, l)
    tk = min(SB_KEY_TILE, l)
    qmap = lambda b, hp, i: (b, i, hp)
    kmap = lambda b, hp, i: (b, hp, 0)
    out = pl.pallas_call(
        functools.partial(_sb_prompt_kernel, tq=tq, tk=tk, rc=SB_ROW_CHUNK),
        grid=(n, C_HEADS // 2, l // tq),
        in_specs=[pl.BlockSpec(memory_space=pltpu.SMEM),
                  pl.BlockSpec((1, tq, HEAD_PAIR), qmap),
                  pl.BlockSpec((1, HEAD_PAIR, l), kmap),
                  pl.BlockSpec((1, HEAD_PAIR, l), kmap)],
        out_specs=pl.BlockSpec((1, tq, HEAD_PAIR), qmap),
        out_shape=jax.ShapeDtypeStruct((n, l, C_WIDTH), BF16),
        scratch_shapes=[pltpu.VMEM((2, tq, HEAD_PAIR), BF16),
                        pltpu.VMEM((2, tq, tk), F32),
                        pltpu.VMEM((2, tq, HEAD_PAIR), F32),
                        pltpu.VMEM((2, tq, HEAD_PAIR), F32)],
        compiler_params=_cparams(("parallel", "parallel", "parallel")),
        name="sb_prompt",
    )(bias, qb, kt, vt)
    return out.reshape(n * l, C_WIDTH)


def _sb_sample_kernel(pt_ref, q_ref, kn_ref, vn_ref, bias_ref, *refs, lq, pages):
    del pt_ref
    kp_refs = refs[:pages]
    vp_refs = refs[pages:2 * pages]
    o_ref, qbd_ref, acc_ref, c_ref = refs[2 * pages:]
    p = pl.program_id(1)
    rows = lq * C_HEADS
    r16 = lax.broadcasted_iota(jnp.int32, (C_HEADS, C_WIDTH), 0)
    c16 = lax.broadcasted_iota(jnp.int32, (C_HEADS, C_WIDTH), 1)
    own = r16 == c16 // C_HEAD_DIM
    upper = _newer_keys_matrix(PAGE_SIZE)
    bias = bias_ref[...]

    def tiles(k2s, v2s, mask):
        z = _dot(qbd_ref[...], jnp.concatenate(k2s, axis=1)) + bias
        staged = [_sb_gate_stage(z[:, j * PAGE_SIZE:(j + 1) * PAGE_SIZE], upper, mask)
                  for j in range(len(k2s))]
        c = c_ref[...]
        ws = []
        for log_beta, rest, tot in staged:
            ws.append(_sb_weight_stage(log_beta, rest, c, mask))
            c = c + tot
        acc_ref[...] += _dot_nt(jnp.concatenate(ws, axis=1), jnp.concatenate(v2s, axis=1))
        c_ref[...] = c

    @pl.when(p == 0)
    def _():
        q = q_ref[0]
        for tq in range(lq):
            blk = jnp.broadcast_to(q[tq:tq + 1, :], (C_HEADS, C_WIDTH))
            qbd_ref[tq * C_HEADS:(tq + 1) * C_HEADS, :] = jnp.where(own, blk, 0.0).astype(BF16)
        acc_ref[...] = jnp.zeros_like(acc_ref)
        c_ref[...] = jnp.zeros_like(c_ref)
        qrow = lax.broadcasted_iota(jnp.int32, (rows, PAGE_SIZE), 0) // C_HEADS
        kcol = lax.broadcasted_iota(jnp.int32, (rows, PAGE_SIZE), 1)
        tiles([kn_ref[0]], [vn_ref[0]], kcol < qrow)

    @pl.when(p > 0)
    def _():
        tiles([r[0].astype(BF16) for r in kp_refs], [r[0].astype(BF16) for r in vp_refs], None)

    @pl.when(p == pl.num_programs(1) - 1)
    def _():
        for tq in range(lq):
            blk = acc_ref[tq * C_HEADS:(tq + 1) * C_HEADS, :]
            o_ref[0, tq:tq + 1, :] = jnp.sum(jnp.where(own, blk, 0.0), axis=0,
                                             keepdims=True).astype(o_ref.dtype)


SB_PAGES_PER_STEP = 8


def _sb_sample(q, k, v, pool_k, pool_v, page_table, bias, n, lq):
    n_pages = page_table.shape[1]
    n_pool = pool_k.shape[0]
    rows = lq * C_HEADS
    q3 = q.reshape(n, lq, C_WIDTH)
    pad = ((0, 0), (0, 0), (0, PAGE_SIZE - lq))
    cols = lambda t: jnp.pad(t.reshape(n, lq, C_WIDTH).transpose(0, 2, 1).astype(BF16), pad)
    kn = cols(k)
    vn = cols(v)
    pages = lambda t: t.transpose(0, 2, 3, 1).reshape(n_pool, C_WIDTH, PAGE_SIZE)
    kp = pages(pool_k)
    vp = pages(pool_v)
    bias_col = jnp.tile(bias.astype(F32) * LOG2E, lq).reshape(rows, 1)
    seq = lambda b, p, pt: (b, 0, 0)
    pages = SB_PAGES_PER_STEP if n_pages % SB_PAGES_PER_STEP == 0 else 1

    def page(j):
        def index(b, p, pt):
            return pt[b, n_pages - 1 - (jnp.maximum(p - 1, 0) * pages + j)], 0, 0
        return pl.BlockSpec((1, C_WIDTH, PAGE_SIZE), index)

    grid_spec = pltpu.PrefetchScalarGridSpec(
        num_scalar_prefetch=1,
        grid=(n, n_pages // pages + 1),
        in_specs=[pl.BlockSpec((1, lq, C_WIDTH), seq),
                  pl.BlockSpec((1, C_WIDTH, PAGE_SIZE), seq),
                  pl.BlockSpec((1, C_WIDTH, PAGE_SIZE), seq),
                  pl.BlockSpec((rows, 1), lambda b, p, pt: (0, 0))]
        + [page(j) for j in range(pages)] * 2,
        out_specs=pl.BlockSpec((1, lq, C_WIDTH), seq),
        scratch_shapes=[pltpu.VMEM((rows, C_WIDTH), BF16),
                        pltpu.VMEM((rows, C_WIDTH), F32),
                        pltpu.VMEM((rows, 1), F32)],
    )
    out = pl.pallas_call(
        functools.partial(_sb_sample_kernel, lq=lq, pages=pages),
        grid_spec=grid_spec,
        out_shape=jax.ShapeDtypeStruct((n, lq, C_WIDTH), F32),
        compiler_params=_cparams(("parallel", "arbitrary")),
        name="sb_sample",
    )(page_table, q3, kn, vn, bias_col, *([kp] * pages), *([vp] * pages))
    return out.reshape(n * lq, C_WIDTH).astype(BF16)


def _c_out_kernel(x_ref, a_ref, wo_ref, g_ref, b_ref, o_ref):
    y = _dot(a_ref[...], wo_ref[...])
    o_ref[...] = _layer_norm(DN_ALPHA * x_ref[...] + y, g_ref[...], b_ref[...])


def _c_out(x, a, w_out, g, b):
    m = x.shape[0]
    tm = min(512, m)
    row = lambda i: (i, 0)
    fixed = lambda i: (0, 0)
    return pl.pallas_call(
        _c_out_kernel,
        grid=(m // tm,),
        in_specs=[pl.BlockSpec((tm, D_MODEL), row),
                  pl.BlockSpec((tm, C_WIDTH), row),
                  pl.BlockSpec(w_out.shape, fixed),
                  pl.BlockSpec((1, D_MODEL), fixed),
                  pl.BlockSpec((1, D_MODEL), fixed)],
        out_specs=pl.BlockSpec((tm, D_MODEL), row),
        out_shape=jax.ShapeDtypeStruct((m, D_MODEL), F32),
        compiler_params=_cparams(("parallel",)),
        name="c_out",
    )(x, a, w_out, g, b)


def _mixer_ab(hp, hs, np_, lp, ns, ls, state_b, w_in, w_out, a_w_s, a_b_s, a_ln_g, a_ln_b,
              lb_logits, norm_g, ln_g, ln_b):
    w_in = w_in.astype(BF16)
    w_out = w_out.astype(BF16)
    a_ln_g = a_ln_g.reshape(1, A_WIDTH)
    a_ln_b = a_ln_b.reshape(1, A_WIDTH)
    norm_g = norm_g.reshape(1, B_HEAD_DIM)
    ln_g = ln_g.reshape(1, D_MODEL)
    ln_b = ln_b.reshape(1, D_MODEL)

    u, vh, q, lf, kk, vi, gate = _ab_in(hp, w_in, a_ln_g, a_ln_b, lb_logits)
    seq = lambda t: t.reshape(np_, lp, B_WIDTH)
    s0 = jnp.zeros((np_, B_HEADS, B_HEAD_DIM, B_HEAD_DIM), F32)
    ob, sp = _hgrn(seq(q), seq(kk), seq(vi), seq(lf), seq(gate), s0, norm_g, chunk=128)
    lanes = lambda bcol: jnp.broadcast_to(bcol[:, :, None], bcol.shape + (A_HEAD_DIM,))
    xp = _ab_out(hp, u, vh, ob.reshape(np_ * lp, B_WIDTH), a_w_s, lanes(a_b_s), w_out, ln_g, ln_b)

    u, vh, q, lf, kk, vi, gate = _ab_in(hs, w_in, a_ln_g, a_ln_b, lb_logits)
    pad = lambda t: jnp.pad(t.reshape(ns, ls, B_WIDTH), ((0, 0), (0, SUB - ls), (0, 0)))
    ob, ss = _hgrn(pad(q), pad(kk), pad(vi), pad(lf), pad(gate), state_b, norm_g, chunk=SUB)
    ob = ob[:, :ls].reshape(ns * ls, B_WIDTH)
    eye = jnp.eye(ns, dtype=F32)
    ws_blk = jnp.einsum("ab,hts->hatbs", eye, a_w_s[:, :ls, :ls]).reshape(A_HEADS, ns * ls, ns * ls)
    bs_blk = jnp.tile(a_b_s[:, :ls], (1, ns))
    xs = _ab_out(hs, u, vh, ob, ws_blk, lanes(bs_blk), w_out, ln_g, ln_b)
    return xp, xs, vh.reshape(ns, ls, A_WIDTH), sp, ss


def _mixer_c(hp, hs, np_, lp, ns, ls, pool_k, pool_v, page_table, w_in, w_out, bias, ln_g, ln_b):
    w_in = w_in.astype(BF16)
    w_out = w_out.astype(BF16)
    ln_g = ln_g.reshape(1, D_MODEL)
    ln_b = ln_b.reshape(1, D_MODEL)
    bias = bias.astype(F32)
    wq = w_in[:, :C_WIDTH]
    wkt = w_in[:, C_WIDTH:2 * C_WIDTH].T
    wvt = w_in[:, 2 * C_WIDTH:].T
    qb, kt, vt = _qkv_cols(hp.reshape(np_, lp, D_MODEL), wq, wkt, wvt)
    ap = _sb_prompt(qb, kt, vt, bias)
    xp = _c_out(hp, ap, w_out, ln_g, ln_b)
    q, kn, vn = _qkv_rows(hs, w_in)
    a_s = _sb_sample(q, kn, vn, pool_k, pool_v, page_table, bias, ns, ls)
    xs = _c_out(hs, a_s, w_out, ln_g, ln_b)
    heads = lambda t: t.reshape(ns, ls, C_HEADS, C_HEAD_DIM)
    pos_major = lambda t: t.reshape(np_, C_HEADS, C_HEAD_DIM, lp).transpose(0, 3, 1, 2)
    return xp, xs, pos_major(kt), pos_major(vt), heads(kn), heads(vn)


def kernel(x_prompt, x_sample, state_b, cache_k, cache_v, page_table, ab_w_in, ab_w_out, a_w_s, a_b_s, a_ln_g, a_ln_b, b_lb_logits, b_norm_g, c_w_in, c_w_out, c_logit_bias, ln1_g, ln1_b, ln2_g, ln2_b, ffn_w_gate, ffn_w_up, ffn_w_down):
    np_, lp, _ = x_prompt.shape
    ns, ls, _ = x_sample.shape
    hp = x_prompt.reshape(np_ * lp, D_MODEL)
    hs = x_sample.reshape(ns * ls, D_MODEL)
    a_v_s, b_p, b_s, k_p, v_p, k_s, v_s = [], [], [], [], [], [], []
    for layer in range(DEPTH):
        i = layer // 2
        if layer % 2 == 0:
            hp, hs, av, sp, ss = _mixer_ab(
                hp, hs, np_, lp, ns, ls, state_b[i], ab_w_in[i], ab_w_out[i], a_w_s[i], a_b_s[i],
                a_ln_g[i], a_ln_b[i], b_lb_logits.astype(F32), b_norm_g[i], ln1_g[layer], ln1_b[layer])
            a_v_s.append(av)
            b_p.append(sp)
            b_s.append(ss)
        else:
            hp, hs, kp, vp, kn, vn = _mixer_c(
                hp, hs, np_, lp, ns, ls, cache_k[i], cache_v[i], page_table, c_w_in[i], c_w_out[i],
                c_logit_bias[i], ln1_g[layer], ln1_b[layer])
            k_p.append(kp)
            v_p.append(vp)
            k_s.append(kn)
            v_s.append(vn)
        wg = ffn_w_gate[layer].astype(BF16)
        wu = ffn_w_up[layer].astype(BF16)
        wd = ffn_w_down[layer].astype(BF16)
        g2 = ln2_g[layer].reshape(1, D_MODEL)
        b2 = ln2_b[layer].reshape(1, D_MODEL)
        hp = _ffn(hp, wg, wu, wd, g2, b2)
        hs = _ffn(hs, wg, wu, wd, g2, b2)
    return (hp.reshape(np_, lp, D_MODEL), hs.reshape(ns, ls, D_MODEL), jnp.stack(a_v_s),
            jnp.stack(b_p), jnp.stack(b_s), jnp.stack(k_p), jnp.stack(v_p), jnp.stack(k_s),
            jnp.stack(v_s))
```

```python
import functools

import jax
import jax.numpy as jnp
from jax import lax
from jax.experimental import pallas as pl
from jax.experimental.pallas import tpu as pltpu

F32 = jnp.float32
BF16 = jnp.bfloat16

D_MODEL = 1024
DEPTH = 2
PAGE_SIZE = 128
A_HEADS = 4
A_HEAD_DIM = 128
A_WIDTH = A_HEADS * A_HEAD_DIM
A_CHUNK = 128
B_HEADS = 4
B_HEAD_DIM = 128
B_WIDTH = B_HEADS * B_HEAD_DIM
C_HEADS = 16
C_HEAD_DIM = 64
C_WIDTH = C_HEADS * C_HEAD_DIM
FFN_HIDDEN = 2816
DN_ALPHA = (2 * DEPTH) ** 0.25
LN_EPS = 1e-5
RMS_EPS = 1e-6
LOG2E = 1.4426950408889634
Q_SCALE = C_HEAD_DIM ** -0.5 * LOG2E
HEAD_PAIR = 2 * C_HEAD_DIM
SUB = 16
VMEM_LIMIT = 56 * 1024 * 1024


def _cparams(sem):
    return pltpu.CompilerParams(dimension_semantics=sem, vmem_limit_bytes=VMEM_LIMIT)


def _sigmoid(x):
    return 1.0 / (1.0 + jnp.exp(-x))


def _silu(x):
    return x * _sigmoid(x)


def _gelu_tanh(x):
    return 0.5 * x * (1.0 + jnp.tanh(0.7978845608028654 * (x + 0.044715 * (x * x * x))))


def _layer_norm(x, g, b):
    mu = jnp.mean(x, axis=-1, keepdims=True)
    xc = x - mu
    var = jnp.mean(xc * xc, axis=-1, keepdims=True)
    return xc * lax.rsqrt(var + LN_EPS) * g + b


def _split3(x):
    hi = x.astype(BF16)
    r = x - hi.astype(F32)
    mid = r.astype(BF16)
    lo = (r - mid.astype(F32)).astype(BF16)
    return hi, mid, lo


def _dot(a, b):
    return jnp.dot(a, b, preferred_element_type=F32)


def _dot_nt(a, b):
    return lax.dot_general(a, b, (((1,), (1,)), ((), ())), preferred_element_type=F32)


def _dot_tn(a, b):
    return lax.dot_general(a, b, (((0,), (0,)), ((), ())), preferred_element_type=F32)


def _ab_in_kernel(h_ref, w_ref, lng_ref, lnb_ref, lbl_ref,
                  u_ref, vh_ref, q_ref, lf_ref, k_ref, vi_ref, g_ref):
    hb = h_ref[...].astype(BF16)

    def proj(j):
        return _dot(hb, w_ref[:, j * A_WIDTH:(j + 1) * A_WIDTH])

    u_ref[...] = _gelu_tanh(proj(0))
    v = _gelu_tanh(proj(1))
    for hd in range(A_HEADS):
        sl = slice(hd * A_HEAD_DIM, (hd + 1) * A_HEAD_DIM)
        vh_ref[:, sl] = _layer_norm(v[:, sl], lng_ref[:, sl], lnb_ref[:, sl])
    q_ref[...] = _silu(proj(2))
    lg = lbl_ref[...]
    ex = jnp.exp(lg - jnp.max(lg, axis=0, keepdims=True))
    lb = ex[0:1, :] / jnp.sum(ex, axis=0, keepdims=True)
    fp = proj(3)
    lf_ref[...] = jnp.log(lb + (1.0 - lb) * _sigmoid(fp))
    k_ref[...] = (1.0 - lb) * _sigmoid(-fp)
    vi_ref[...] = proj(4)
    g_ref[...] = _silu(proj(5))


def _ab_in(h, w_in, ln_g, ln_b, lb_logits):
    m = h.shape[0]
    tm = min(256, m)
    row = lambda i: (i, 0)
    fixed = lambda i: (0, 0)
    outs = [jax.ShapeDtypeStruct((m, A_WIDTH), F32)] * 7
    return pl.pallas_call(
        _ab_in_kernel,
        grid=(m // tm,),
        in_specs=[pl.BlockSpec((tm, D_MODEL), row),
                  pl.BlockSpec(w_in.shape, fixed),
                  pl.BlockSpec((1, A_WIDTH), fixed),
                  pl.BlockSpec((1, A_WIDTH), fixed),
                  pl.BlockSpec(lb_logits.shape, fixed)],
        out_specs=[pl.BlockSpec((tm, A_WIDTH), row)] * 7,
        out_shape=outs,
        compiler_params=_cparams(("parallel",)),
        name="ab_in",
    )(h, w_in, ln_g, ln_b, lb_logits)


def _hgrn_chunk(q, kk, vi, lf, st, ltri):
    c = q.shape[0]
    nsub = c // SUB
    hi, mid, lo = _split3(lf)
    g = _dot(ltri, hi) + _dot(ltri, mid) + _dot(ltri, lo)
    g_last = g[c - 1:c, :]
    o_inter = _dot_nt((q * jnp.exp(g)).astype(BF16), st.astype(BF16))
    kd = kk * jnp.exp(g_last - g)
    st_new = st * jnp.exp(g_last) + _dot_tn(vi.astype(BF16), kd.astype(BF16))
    o_blocks = [o_inter[i * SUB:(i + 1) * SUB] for i in range(nsub)]
    for j in range(nsub - 1):
        i0 = SUB * (j + 1)
        r = g[i0 - 1:i0, :]
        a = (q[i0:] * jnp.exp(g[i0:] - r)).astype(BF16)
        b = (kk[i0 - SUB:i0] * jnp.exp(r - g[i0 - SUB:i0])).astype(BF16)
        sc = _dot_nt(a, b).astype(BF16)
        res = _dot(sc, vi[i0 - SUB:i0].astype(BF16))
        for i in range(j + 1, nsub):
            o_blocks[i] = o_blocks[i] + res[(i - j - 1) * SUB:(i - j) * SUB]
    half = SUB // 2
    tid = lax.broadcasted_iota(jnp.int32, (half, q.shape[1]), 0)
    for jb in range(nsub):
        base = jb * SUB
        groups = []
        for rg in range(2):
            lo_r = base + rg * half
            gt = g[lo_r:lo_r + half]
            qt = q[lo_r:lo_r + half]
            acc = jnp.zeros_like(gt)
            for s in range(half * (rg + 1)):
                gs = g[base + s:base + s + 1]
                diff = gt - gs
                if s >= rg * half:
                    diff = jnp.where(tid >= s - rg * half, diff, -jnp.inf)
                p = qt * jnp.exp(diff) * kk[base + s:base + s + 1]
                cs = jnp.sum(p, axis=-1, keepdims=True)
                acc = acc + cs * vi[base + s:base + s + 1]
            groups.append(acc)
        o_blocks[jb] = o_blocks[jb] + jnp.concatenate(groups, axis=0)
    return jnp.concatenate(o_blocks, axis=0), st_new


def _hgrn_kernel(q_ref, k_ref, v_ref, lf_ref, gate_ref, s0_ref, ng_ref, o_ref, s_ref, st_ref,
                 *, chunk, nchunks):
    i = pl.program_id(2)

    @pl.when(i == 0)
    def _():
        st_ref[...] = s0_ref[0, 0].T

    row = lax.broadcasted_iota(jnp.int32, (chunk, chunk), 0)
    col = lax.broadcasted_iota(jnp.int32, (chunk, chunk), 1)
    ltri = (row >= col).astype(BF16)
    ng = ng_ref[...]

    def body(ci, carry):
        r0 = pl.multiple_of(ci * chunk, chunk)
        rows = pl.ds(r0, chunk)
        o, st_new = _hgrn_chunk(q_ref[0, rows, :], k_ref[0, rows, :], v_ref[0, rows, :],
                                lf_ref[0, rows, :], st_ref[...], ltri)
        st_ref[...] = st_new
        o = o * lax.rsqrt(jnp.mean(o * o, axis=-1, keepdims=True) + RMS_EPS) * ng
        o_ref[0, rows, :] = o * gate_ref[0, rows, :]
        return carry

    lax.fori_loop(0, nchunks, body, 0)

    @pl.when(i == pl.num_programs(2) - 1)
    def _():
        s_ref[0, 0] = st_ref[...].T


def _hgrn(q, kk, vi, lf, gate, s0, norm_g, chunk):
    n, l, _ = q.shape
    tb = min(l, 4 * chunk)
    seq = lambda b, h, i: (b, i, h)
    st = lambda b, h, i: (b, h, 0, 0)
    blk = pl.BlockSpec((1, tb, B_HEAD_DIM), seq)
    kern = functools.partial(_hgrn_kernel, chunk=chunk, nchunks=tb // chunk)
    return pl.pallas_call(
        kern,
        grid=(n, B_HEADS, l // tb),
        in_specs=[blk, blk, blk, blk, blk,
                  pl.BlockSpec((1, 1, B_HEAD_DIM, B_HEAD_DIM), st),
                  pl.BlockSpec((1, B_HEAD_DIM), lambda b, h, i: (0, 0))],
        out_specs=[blk, pl.BlockSpec((1, 1, B_HEAD_DIM, B_HEAD_DIM), st)],
        out_shape=[jax.ShapeDtypeStruct((n, l, B_WIDTH), F32),
                   jax.ShapeDtypeStruct((n, B_HEADS, B_HEAD_DIM, B_HEAD_DIM), F32)],
        scratch_shapes=[pltpu.VMEM((B_HEAD_DIM, B_HEAD_DIM), F32)],
        compiler_params=_cparams(("parallel", "parallel", "arbitrary")),
        name="hgrn2",
    )(q, kk, vi, lf, gate, s0, norm_g)


def _ab_out_kernel(x_ref, u_ref, vh_ref, ob_ref, ws_ref, bs_ref, wo_ref, g_ref, b_ref,
                   o_ref, a_ref):
    tm = x_ref.shape[0]
    row = lax.broadcasted_iota(jnp.int32, (A_CHUNK, A_CHUNK), 0)
    col = lax.broadcasted_iota(jnp.int32, (A_CHUNK, A_CHUNK), 1)
    causal = row >= col
    for hd in range(A_HEADS):
        w = jnp.where(causal, ws_ref[hd], 0.0).astype(BF16)
        cs = slice(hd * A_HEAD_DIM, (hd + 1) * A_HEAD_DIM)
        for c in range(tm // A_CHUNK):
            rs = slice(c * A_CHUNK, (c + 1) * A_CHUNK)
            sv = _dot(w, vh_ref[rs, cs].astype(BF16)) + bs_ref[hd]
            a_ref[rs, cs] = (u_ref[rs, cs] * sv).astype(BF16)
    y = _dot(a_ref[...], wo_ref[0:A_WIDTH, :]) + _dot(ob_ref[...].astype(BF16), wo_ref[A_WIDTH:, :])
    o_ref[...] = _layer_norm(DN_ALPHA * x_ref[...] + y, g_ref[...], b_ref[...])


def _ab_out(x, u, vh, ob, ws, bs, w_out, g, b):
    m = x.shape[0]
    tm = min(512, m)
    row = lambda i: (i, 0)
    fixed2 = lambda i: (0, 0)
    return pl.pallas_call(
        _ab_out_kernel,
        grid=(m // tm,),
        in_specs=[pl.BlockSpec((tm, D_MODEL), row),
                  pl.BlockSpec((tm, A_WIDTH), row),
                  pl.BlockSpec((tm, A_WIDTH), row),
                  pl.BlockSpec((tm, B_WIDTH), row),
                  pl.BlockSpec(ws.shape, lambda i: (0, 0, 0)),
                  pl.BlockSpec(bs.shape, lambda i: (0, 0, 0)),
                  pl.BlockSpec(w_out.shape, fixed2),
                  pl.BlockSpec((1, D_MODEL), fixed2),
                  pl.BlockSpec((1, D_MODEL), fixed2)],
        out_specs=pl.BlockSpec((tm, D_MODEL), row),
        out_shape=jax.ShapeDtypeStruct((m, D_MODEL), F32),
        scratch_shapes=[pltpu.VMEM((tm, A_WIDTH), BF16)],
        compiler_params=_cparams(("parallel",)),
        name="ab_out",
    )(x, u, vh, ob, ws, bs, w_out, g, b)


def _ffn_kernel(x_ref, wg_ref, wu_ref, wd_ref, g_ref, b_ref, o_ref, *, th):
    x = x_ref[...]
    xb = x.astype(BF16)
    y = jnp.zeros(x.shape, F32)
    for j in range(FFN_HIDDEN // th):
        cs = slice(j * th, (j + 1) * th)
        gt = _dot(xb, wg_ref[:, cs])
        up = _dot(xb, wu_ref[:, cs])
        y = y + _dot((_silu(gt) * up).astype(BF16), wd_ref[cs, :])
    o_ref[...] = _layer_norm(DN_ALPHA * x + y, g_ref[...], b_ref[...])


def _ffn(x, wg, wu, wd, g, b):
    m = x.shape[0]
    tm = min(512, m)
    row = lambda i: (i, 0)
    fixed = lambda i: (0, 0)
    once = pl.Buffered(1)
    return pl.pallas_call(
        functools.partial(_ffn_kernel, th=FFN_HIDDEN // 2),
        grid=(m // tm,),
        in_specs=[pl.BlockSpec((tm, D_MODEL), row),
                  pl.BlockSpec(wg.shape, fixed, pipeline_mode=once),
                  pl.BlockSpec(wu.shape, fixed, pipeline_mode=once),
                  pl.BlockSpec(wd.shape, fixed, pipeline_mode=once),
                  pl.BlockSpec((1, D_MODEL), fixed),
                  pl.BlockSpec((1, D_MODEL), fixed)],
        out_specs=pl.BlockSpec((tm, D_MODEL), row),
        out_shape=jax.ShapeDtypeStruct((m, D_MODEL), F32),
        compiler_params=_cparams(("parallel",)),
        name="ffn",
    )(x, wg, wu, wd, g, b)


def _qkv_rows_kernel(h_ref, w_ref, q_ref, k_ref, v_ref):
    hb = h_ref[...].astype(BF16)
    q_ref[...] = _dot(hb, w_ref[:, 0:C_WIDTH]) * Q_SCALE
    k_ref[...] = _dot(hb, w_ref[:, C_WIDTH:2 * C_WIDTH])
    v_ref[...] = _dot(hb, w_ref[:, 2 * C_WIDTH:])


def _qkv_rows(h, w_in):
    m = h.shape[0]
    tm = min(512, m)
    row = lambda i: (i, 0)
    blk = pl.BlockSpec((tm, C_WIDTH), row)
    return pl.pallas_call(
        _qkv_rows_kernel,
        grid=(m // tm,),
        in_specs=[pl.BlockSpec((tm, D_MODEL), row), pl.BlockSpec(w_in.shape, lambda i: (0, 0))],
        out_specs=[blk] * 3,
        out_shape=[jax.ShapeDtypeStruct((m, C_WIDTH), F32)] * 3,
        compiler_params=_cparams(("parallel",)),
        name="qkv_rows",
    )(h, w_in)


def _qkv_cols_kernel(h_ref, wq_ref, wkt_ref, wvt_ref, qb_ref, kt_ref, vt_ref):
    hb = h_ref[0].astype(BF16)
    qb_ref[0] = (_dot(hb, wq_ref[...]) * Q_SCALE).astype(BF16)
    kt_ref[0] = _dot_nt(wkt_ref[...], hb)
    vt_ref[0] = _dot_nt(wvt_ref[...], hb)


def _qkv_cols(h, wq, wkt, wvt):
    n, l, _ = h.shape
    tm = min(512, l)
    row = lambda b, i: (b, i, 0)
    colmap = lambda b, i: (b, 0, i)
    fixed = lambda b, i: (0, 0)
    wspec = pl.BlockSpec((C_WIDTH, D_MODEL), fixed)
    return pl.pallas_call(
        _qkv_cols_kernel,
        grid=(n, l // tm),
        in_specs=[pl.BlockSpec((1, tm, D_MODEL), row), pl.BlockSpec((D_MODEL, C_WIDTH), fixed),
                  wspec, wspec],
        out_specs=[pl.BlockSpec((1, tm, C_WIDTH), row),
                   pl.BlockSpec((1, C_WIDTH, tm), colmap),
                   pl.BlockSpec((1, C_WIDTH, tm), colmap)],
        out_shape=[jax.ShapeDtypeStruct((n, l, C_WIDTH), BF16),
                   jax.ShapeDtypeStruct((n, C_WIDTH, l), F32),
                   jax.ShapeDtypeStruct((n, C_WIDTH, l), F32)],
        compiler_params=_cparams(("parallel", "parallel")),
        name="qkv_cols",
    )(h, wq, wkt, wvt)


Z_MAX = 126.0


def _sb_gate_stage(z, upper, mask):
    log_1m = jnp.log(1.0 + jnp.exp2(z)) * (-LOG2E)
    log_beta = z + log_1m
    if mask is not None:
        log_1m = jnp.where(mask, log_1m, 0.0)
        log_beta = jnp.where(mask, log_beta, -jnp.inf)
    rest = _dot(log_1m.astype(BF16), upper)
    return log_beta, rest, jnp.sum(log_1m, axis=-1, keepdims=True)


def _sb_weight_stage(log_beta, rest, carry):
    return jnp.exp2(log_beta + rest + carry).astype(BF16)


def _newer_keys_matrix(tk):
    row = lax.broadcasted_iota(jnp.int32, (tk, tk), 0)
    col = lax.broadcasted_iota(jnp.int32, (tk, tk), 1)
    return (row > col).astype(BF16)


def _sb_prompt_kernel(bias_ref, q_ref, k_ref, v_ref, o_ref, qm_ref, z_ref, lb_ref, rest_ref, tot_ref,
                      acc_ref, c_ref, *, tq, tk, rc):
    hp = pl.program_id(1)
    qi = pl.program_id(2)
    nd = tq // tk
    newest = (qi + 1) * nd - 1
    chunks = [(r, h) for r in range(tq // rc) for h in range(2)]
    lanes = tk // HEAD_PAIR
    q2 = q_ref[0]
    first = lax.broadcasted_iota(jnp.int32, q2.shape, 1) < C_HEAD_DIM
    zero = jnp.zeros_like(q2)
    qm_ref[0] = jnp.where(first, q2, zero)
    qm_ref[1] = jnp.where(first, zero, q2)
    upper = _newer_keys_matrix(tk)
    ahead = (lax.broadcasted_iota(jnp.int32, (rc, tk), 1)
             - lax.broadcasted_iota(jnp.int32, (rc, tk), 0))
    bias2 = [bias_ref[2 * hp + h] * LOG2E for h in range(2)]
    acc_ref[...] = jnp.zeros_like(acc_ref)
    c_ref[...] = jnp.zeros_like(c_ref)

    def key_cols(kt):
        return pl.ds(pl.multiple_of(kt * tk, tk), tk)

    def keys(kt):
        return k_ref[0, :, key_cols(kt)].astype(BF16)

    def values(kt):
        return v_ref[0, :, key_cols(kt)].astype(BF16)

    def logits(k2, r, h):
        rows = slice(r * rc, (r + 1) * rc)
        z_ref[h, rows, :] = jnp.minimum(_dot(qm_ref[h, rows, :], k2) + bias2[h], Z_MAX)

    def weigh(v2, r, h):
        rows = slice(r * rc, (r + 1) * rc)
        c = c_ref[h, rows, :]
        w = _sb_weight_stage(lb_ref[h, rows, :], rest_ref[h, rows, :], jnp.tile(c, (1, lanes)))
        acc_ref[h, rows, :] += _dot_nt(w, v2)
        c_ref[h, rows, :] = c + tot_ref[h, rows, :]

    def visible(row_off, r):
        if row_off is None or row_off + r * rc > tk - 1:
            return "all"
        return "none" if row_off + r * rc <= -(rc - 1) else "some"

    def tile(kt, row_off, parked):
        v2_prev = values(jnp.minimum(kt + 1, newest))
        k2_next = keys(jnp.maximum(kt - 1, 0))
        gated = []
        for r, h in chunks:
            rows = slice(r * rc, (r + 1) * rc)
            see = visible(row_off, r)
            if see != "none":
                mask = None if see == "all" else ahead < row_off + r * rc
                log_beta, rest, tot = _sb_gate_stage(z_ref[h, rows, :], upper, mask)
            if (r, h) in parked:
                weigh(v2_prev, r, h)
            if see != "none":
                lb_ref[h, rows, :] = log_beta
                rest_ref[h, rows, :] = rest
                tot_ref[h, rows, :] = jnp.broadcast_to(tot, (rc, HEAD_PAIR))
                gated.append((r, h))
            logits(k2_next, r, h)
        return gated

    k2_first = keys(newest)
    for r, h in chunks:
        logits(k2_first, r, h)
    parked = []
    for d in range(nd):
        parked = tile(newest - d, (d + 1 - nd) * tk, parked)

    def older(j, carry):
        tile(qi * nd - 1 - j, None, chunks)
        return carry

    lax.fori_loop(0, qi * nd, older, 0)
    v2_last = values(0)
    for r, h in chunks:
        weigh(v2_last, r, h)
    o_ref[0] = jnp.where(first, acc_ref[0], acc_ref[1]).astype(o_ref.dtype)


SB_QUERY_TILE = 512
SB_KEY_TILE = 256
SB_ROW_CHUNK = 128


def _sb_prompt(qb, kt, vt, bias):
    n, l, _ = qb.shape
    tq = min(SB_QUERY_TILE, l)
    tk = min(SB_KEY_TILE, l)
    qmap = lambda b, hp, i: (b, i, hp)
    kmap = lambda b, hp, i: (b, hp, 0)
    out = pl.pallas_call(
        functools.partial(_sb_prompt_kernel, tq=tq, tk=tk, rc=SB_ROW_CHUNK),
        grid=(n, C_HEADS // 2, l // tq),
        in_specs=[pl.BlockSpec(memory_space=pltpu.SMEM),
                  pl.BlockSpec((1, tq, HEAD_PAIR), qmap),
                  pl.BlockSpec((1, HEAD_PAIR, l), kmap),
                  pl.BlockSpec((1, HEAD_PAIR, l), kmap)],
        out_specs=pl.BlockSpec((1, tq, HEAD_PAIR), qmap),
        out_shape=jax.ShapeDtypeStruct((n, l, C_WIDTH), BF16),
        scratch_shapes=[pltpu.VMEM((2, tq, HEAD_PAIR), BF16),
                        pltpu.VMEM((2, tq, tk), F32),
                        pltpu.VMEM((2, tq, tk), F32),
                        pltpu.VMEM((2, tq, tk), F32),
                        pltpu.VMEM((2, tq, HEAD_PAIR), F32),
                        pltpu.VMEM((2, tq, HEAD_PAIR), F32),
                        pltpu.VMEM((2, tq, HEAD_PAIR), F32)],
        compiler_params=_cparams(("parallel", "parallel", "parallel")),
        name="sb_prompt",
    )(bias, qb, kt, vt)
    return out.reshape(n * l, C_WIDTH)


def _sb_sample_kernel(pt_ref, q_ref, kn_ref, vn_ref, bias_ref, *refs, lq, pages):
    del pt_ref
    kp_refs = refs[:pages]
    vp_refs = refs[pages:2 * pages]
    o_ref, qbd_ref, acc_ref, c_ref = refs[2 * pages:]
    p = pl.program_id(1)
    rows = lq * C_HEADS
    r16 = lax.broadcasted_iota(jnp.int32, (C_HEADS, C_WIDTH), 0)
    c16 = lax.broadcasted_iota(jnp.int32, (C_HEADS, C_WIDTH), 1)
    own = r16 == c16 // C_HEAD_DIM
    upper = _newer_keys_matrix(PAGE_SIZE)
    bias = bias_ref[...]

    def tiles(k2s, v2s, mask):
        z = _dot(qbd_ref[...], jnp.concatenate(k2s, axis=1)) + bias
        staged = [_sb_gate_stage(z[:, j * PAGE_SIZE:(j + 1) * PAGE_SIZE], upper, mask)
                  for j in range(len(k2s))]
        c = c_ref[...]
        ws = []
        for log_beta, rest, tot in staged:
            ws.append(_sb_weight_stage(log_beta, rest, c))
            c = c + tot
        acc_ref[...] += _dot_nt(jnp.concatenate(ws, axis=1), jnp.concatenate(v2s, axis=1))
        c_ref[...] = c

    @pl.when(p == 0)
    def _():
        q = q_ref[0]
        for tq in range(lq):
            blk = jnp.broadcast_to(q[tq:tq + 1, :], (C_HEADS, C_WIDTH))
            qbd_ref[tq * C_HEADS:(tq + 1) * C_HEADS, :] = jnp.where(own, blk, 0.0).astype(BF16)
        acc_ref[...] = jnp.zeros_like(acc_ref)
        c_ref[...] = jnp.zeros_like(c_ref)
        qrow = lax.broadcasted_iota(jnp.int32, (rows, PAGE_SIZE), 0) // C_HEADS
        kcol = lax.broadcasted_iota(jnp.int32, (rows, PAGE_SIZE), 1)
        tiles([kn_ref[0]], [vn_ref[0]], kcol < qrow)

    @pl.when(p > 0)
    def _():
        tiles([r[0].astype(BF16) for r in kp_refs], [r[0].astype(BF16) for r in vp_refs], None)

    @pl.when(p == pl.num_programs(1) - 1)
    def _():
        for tq in range(lq):
            blk = acc_ref[tq * C_HEADS:(tq + 1) * C_HEADS, :]
            o_ref[0, tq:tq + 1, :] = jnp.sum(jnp.where(own, blk, 0.0), axis=0,
                                             keepdims=True).astype(o_ref.dtype)


SB_PAGES_PER_STEP = 8


def _sb_sample(q, k, v, pool_k, pool_v, page_table, bias, n, lq):
    n_pages = page_table.shape[1]
    n_pool = pool_k.shape[0]
    rows = lq * C_HEADS
    q3 = q.reshape(n, lq, C_WIDTH)
    pad = ((0, 0), (0, 0), (0, PAGE_SIZE - lq))
    cols = lambda t: jnp.pad(t.reshape(n, lq, C_WIDTH).transpose(0, 2, 1).astype(BF16), pad)
    kn = cols(k)
    vn = cols(v)
    pages = lambda t: t.transpose(0, 2, 3, 1).reshape(n_pool, C_WIDTH, PAGE_SIZE)
    kp = pages(pool_k)
    vp = pages(pool_v)
    bias_col = jnp.tile(bias.astype(F32) * LOG2E, lq).reshape(rows, 1)
    seq = lambda b, p, pt: (b, 0, 0)
    pages = SB_PAGES_PER_STEP if n_pages % SB_PAGES_PER_STEP == 0 else 1

    def page(j):
        def index(b, p, pt):
            return pt[b, n_pages - 1 - (jnp.maximum(p - 1, 0) * pages + j)], 0, 0
        return pl.BlockSpec((1, C_WIDTH, PAGE_SIZE), index)

    grid_spec = pltpu.PrefetchScalarGridSpec(
        num_scalar_prefetch=1,
        grid=(n, n_pages // pages + 1),
        in_specs=[pl.BlockSpec((1, lq, C_WIDTH), seq),
                  pl.BlockSpec((1, C_WIDTH, PAGE_SIZE), seq),
                  pl.BlockSpec((1, C_WIDTH, PAGE_SIZE), seq),
                  pl.BlockSpec((rows, 1), lambda b, p, pt: (0, 0))]
        + [page(j) for j in range(pages)] * 2,
        out_specs=pl.BlockSpec((1, lq, C_WIDTH), seq),
        scratch_shapes=[pltpu.VMEM((rows, C_WIDTH), BF16),
                        pltpu.VMEM((rows, C_WIDTH), F32),
                        pltpu.VMEM((rows, 1), F32)],
    )
    out = pl.pallas_call(
        functools.partial(_sb_sample_kernel, lq=lq, pages=pages),
        grid_spec=grid_spec,
        out_shape=jax.ShapeDtypeStruct((n, lq, C_WIDTH), F32),
        compiler_params=_cparams(("parallel", "arbitrary")),
        name="sb_sample",
    )(page_table, q3, kn, vn, bias_col, *([kp] * pages), *([vp] * pages))
    return out.reshape(n * lq, C_WIDTH).astype(BF16)


def _c_out_kernel(x_ref, a_ref, wo_ref, g_ref, b_ref, o_ref):
    y = _dot(a_ref[...], wo_ref[...])
    o_ref[...] = _layer_norm(DN_ALPHA * x_ref[...] + y, g_ref[...], b_ref[...])


def _c_out(x, a, w_out, g, b):
    m = x.shape[0]
    tm = min(512, m)
    row = lambda i: (i, 0)
    fixed = lambda i: (0, 0)
    return pl.pallas_call(
        _c_out_kernel,
        grid=(m // tm,),
        in_specs=[pl.BlockSpec((tm, D_MODEL), row),
                  pl.BlockSpec((tm, C_WIDTH), row),
                  pl.BlockSpec(w_out.shape, fixed),
                  pl.BlockSpec((1, D_MODEL), fixed),
                  pl.BlockSpec((1, D_MODEL), fixed)],
        out_specs=pl.BlockSpec((tm, D_MODEL), row),
        out_shape=jax.ShapeDtypeStruct((m, D_MODEL), F32),
        compiler_params=_cparams(("parallel",)),
        name="c_out",
    )(x, a, w_out, g, b)


def _mixer_ab(hp, hs, np_, lp, ns, ls, state_b, w_in, w_out, a_w_s, a_b_s, a_ln_g, a_ln_b,
              lb_logits, norm_g, ln_g, ln_b):
    w_in = w_in.astype(BF16)
    w_out = w_out.astype(BF16)
    a_ln_g = a_ln_g.reshape(1, A_WIDTH)
    a_ln_b = a_ln_b.reshape(1, A_WIDTH)
    norm_g = norm_g.reshape(1, B_HEAD_DIM)
    ln_g = ln_g.reshape(1, D_MODEL)
    ln_b = ln_b.reshape(1, D_MODEL)

    u, vh, q, lf, kk, vi, gate = _ab_in(hp, w_in, a_ln_g, a_ln_b, lb_logits)
    seq = lambda t: t.reshape(np_, lp, B_WIDTH)
    s0 = jnp.zeros((np_, B_HEADS, B_HEAD_DIM, B_HEAD_DIM), F32)
    ob, sp = _hgrn(seq(q), seq(kk), seq(vi), seq(lf), seq(gate), s0, norm_g, chunk=128)
    lanes = lambda bcol: jnp.broadcast_to(bcol[:, :, None], bcol.shape + (A_HEAD_DIM,))
    xp = _ab_out(hp, u, vh, ob.reshape(np_ * lp, B_WIDTH), a_w_s, lanes(a_b_s), w_out, ln_g, ln_b)

    u, vh, q, lf, kk, vi, gate = _ab_in(hs, w_in, a_ln_g, a_ln_b, lb_logits)
    pad = lambda t: jnp.pad(t.reshape(ns, ls, B_WIDTH), ((0, 0), (0, SUB - ls), (0, 0)))
    ob, ss = _hgrn(pad(q), pad(kk), pad(vi), pad(lf), pad(gate), state_b, norm_g, chunk=SUB)
    ob = ob[:, :ls].reshape(ns * ls, B_WIDTH)
    eye = jnp.eye(ns, dtype=F32)
    ws_blk = jnp.einsum("ab,hts->hatbs", eye, a_w_s[:, :ls, :ls]).reshape(A_HEADS, ns * ls, ns * ls)
    bs_blk = jnp.tile(a_b_s[:, :ls], (1, ns))
    xs = _ab_out(hs, u, vh, ob, ws_blk, lanes(bs_blk), w_out, ln_g, ln_b)
    return xp, xs, vh.reshape(ns, ls, A_WIDTH), sp, ss


def _mixer_c(hp, hs, np_, lp, ns, ls, pool_k, pool_v, page_table, w_in, w_out, bias, ln_g, ln_b):
    w_in = w_in.astype(BF16)
    w_out = w_out.astype(BF16)
    ln_g = ln_g.reshape(1, D_MODEL)
    ln_b = ln_b.reshape(1, D_MODEL)
    bias = bias.astype(F32)
    wq = w_in[:, :C_WIDTH]
    wkt = w_in[:, C_WIDTH:2 * C_WIDTH].T
    wvt = w_in[:, 2 * C_WIDTH:].T
    qb, kt, vt = _qkv_cols(hp.reshape(np_, lp, D_MODEL), wq, wkt, wvt)
    ap = _sb_prompt(qb, kt, vt, bias)
    xp = _c_out(hp, ap, w_out, ln_g, ln_b)
    q, kn, vn = _qkv_rows(hs, w_in)
    a_s = _sb_sample(q, kn, vn, pool_k, pool_v, page_table, bias, ns, ls)
    xs = _c_out(hs, a_s, w_out, ln_g, ln_b)
    heads = lambda t: t.reshape(ns, ls, C_HEADS, C_HEAD_DIM)
    pos_major = lambda t: t.reshape(np_, C_HEADS, C_HEAD_DIM, lp).transpose(0, 3, 1, 2)
    return xp, xs, pos_major(kt), pos_major(vt), heads(kn), heads(vn)


def kernel(x_prompt, x_sample, state_b, cache_k, cache_v, page_table, ab_w_in, ab_w_out, a_w_s, a_b_s, a_ln_g, a_ln_b, b_lb_logits, b_norm_g, c_w_in, c_w_out, c_logit_bias, ln1_g, ln1_b, ln2_g, ln2_b, ffn_w_gate, ffn_w_up, ffn_w_down):
    np_, lp, _ = x_prompt.shape
    ns, ls, _ = x_sample.shape
    hp = x_prompt.reshape(np_ * lp, D_MODEL)
    hs = x_sample.reshape(ns * ls, D_MODEL)
    a_v_s, b_p, b_s, k_p, v_p, k_s, v_s = [], [], [], [], [], [], []
    for layer in range(DEPTH):
        i = layer // 2
        if layer % 2 == 0:
            hp, hs, av, sp, ss = _mixer_ab(
                hp, hs, np_, lp, ns, ls, state_b[i], ab_w_in[i], ab_w_out[i], a_w_s[i], a_b_s[i],
                a_ln_g[i], a_ln_b[i], b_lb_logits.astype(F32), b_norm_g[i], ln1_g[layer], ln1_b[layer])
            a_v_s.append(av)
            b_p.append(sp)
            b_s.append(ss)
        else:
            hp, hs, kp, vp, kn, vn = _mixer_c(
                hp, hs, np_, lp, ns, ls, cache_k[i], cache_v[i], page_table, c_w_in[i], c_w_out[i],
                c_logit_bias[i], ln1_g[layer], ln1_b[layer])
            k_p.append(kp)
            v_p.append(vp)
            k_s.append(kn)
            v_s.append(vn)
        wg = ffn_w_gate[layer].astype(BF16)
        wu = ffn_w_up[layer].astype(BF16)
        wd = ffn_w_down[layer].astype(BF16)
        g2 = ln2_g[layer].reshape(1, D_MODEL)
        b2 = ln2_b[layer].reshape(1, D_MODEL)
        hp = _ffn(hp, wg, wu, wd, g2, b2)
        hs = _ffn(hs, wg, wu, wd, g2, b2)
    return (hp.reshape(np_, lp, D_MODEL), hs.reshape(ns, ls, D_MODEL), jnp.stack(a_v_s),
            jnp.stack(b_p), jnp.stack(b_s), jnp.stack(k_p), jnp.stack(v_p), jnp.stack(k_s),
            jnp.stack(v_s))
```
